```python
import math
import jax, jax.numpy as jnp
from jax import lax
import numpy as np

D_MODEL = 1024
BATCH = 2
SEQ = 16384
DEPTH = 4
DEC_BATCH = 32
DEC_SEQ = 2048
PAST_LEN = 128

N_MIXERS = 3
N_A = (DEPTH + 2) // 3
N_B = (DEPTH + 1) // 3
N_C = DEPTH // 3
FNET_GROUPS = 4
FNET_GROUP_DIM = D_MODEL // FNET_GROUPS
HEAD_DIM = 64
SWA_Q_HEADS = D_MODEL // HEAD_DIM
SWA_KV_HEADS = 4
SWA_GROUP = SWA_Q_HEADS // SWA_KV_HEADS
WINDOW = 128
BLOCK = 128
DIFF_HEADS = D_MODEL // (2 * HEAD_DIM)
D_DIFF = DIFF_HEADS * 2 * HEAD_DIM
D_FF = 2816
CONV_WIDTH = 3
ROPE_THETA = 10000.0
EPS = 1e-6
NEG = -1e30

kernel_name = "hybrid_fnet_swa_diffattn_encoder"


def rms_norm(x, g):
    xf = x.astype(jnp.float32)
    y = xf * lax.rsqrt(jnp.mean(xf * xf, axis=-1, keepdims=True) + EPS)
    return (y * g.astype(jnp.float32)).astype(x.dtype)


def rope_tables(seq):
    inv = 1.0 / (ROPE_THETA ** (jnp.arange(0, HEAD_DIM, 2, dtype=jnp.float32) / HEAD_DIM))
    ang = jnp.arange(seq, dtype=jnp.float32)[:, None] * inv[None, :]
    return jnp.cos(ang), jnp.sin(ang)


def apply_rope(x, cos, sin):
    half = HEAD_DIM // 2
    shp = (1, cos.shape[0]) + (1,) * (x.ndim - 3) + (half,)
    c = cos.reshape(shp)
    s = sin.reshape(shp)
    xf = x.astype(jnp.float32)
    x1, x2 = xf[..., :half], xf[..., half:]
    return jnp.concatenate([x1 * c - x2 * s, x2 * c + x1 * s], axis=-1).astype(x.dtype)


def fourier_mixer(h, w_o, b_o):
    B, S, D = h.shape
    hg = h.astype(jnp.float32).reshape(B, S, FNET_GROUPS, FNET_GROUP_DIM)
    f = jnp.fft.fftn(hg, axes=(1, 3), norm="ortho").real
    return f.reshape(B, S, D).astype(h.dtype) @ w_o + b_o


def windowed_gqa(h, w_qkv, q_g, k_g, sink, w_o, cos, sin):
    B, S, _ = h.shape
    nq = SWA_Q_HEADS * HEAD_DIM
    nkv = SWA_KV_HEADS * HEAD_DIM
    qkv = h @ w_qkv
    q = qkv[..., :nq].reshape(B, S, SWA_KV_HEADS, SWA_GROUP, HEAD_DIM)
    k = qkv[..., nq:nq + nkv].reshape(B, S, SWA_KV_HEADS, HEAD_DIM)
    v = qkv[..., nq + nkv:].reshape(B, S, SWA_KV_HEADS, HEAD_DIM)
    q = apply_rope(rms_norm(q, q_g), cos, sin)
    k = apply_rope(rms_norm(k, k_g), cos, sin)
    pad = ((0, 0), (BLOCK, BLOCK), (0, 0), (0, 0))
    kp = jnp.pad(k, pad)
    vp = jnp.pad(v, pad)
    nb = S // BLOCK
    qb = jnp.moveaxis(q.reshape(B, nb, BLOCK, SWA_KV_HEADS, SWA_GROUP, HEAD_DIM), 1, 0)
    sink_f = sink.astype(jnp.float32).reshape(SWA_KV_HEADS, SWA_GROUP)[None, :, :, None, None]
    scale = HEAD_DIM ** -0.5

    def block(args):
        n, q_blk = args
        start = n * BLOCK
        k_blk = lax.dynamic_slice_in_dim(kp, start, 3 * BLOCK, axis=1)
        v_blk = lax.dynamic_slice_in_dim(vp, start, 3 * BLOCK, axis=1)
        s = jnp.einsum('bqhgd,bkhd->bhgqk', q_blk, k_blk).astype(jnp.float32) * scale
        qi = start + jnp.arange(BLOCK)
        kj = start - BLOCK + jnp.arange(3 * BLOCK)
        valid = (jnp.abs(qi[:, None] - kj[None, :]) <= WINDOW) & (kj >= 0)[None, :] & (kj < S)[None, :]
        s = jnp.where(valid, s, NEG)
        m = jnp.maximum(jnp.max(s, axis=-1, keepdims=True), sink_f)
        e = jnp.exp(s - m)
        denom = jnp.sum(e, axis=-1, keepdims=True) + jnp.exp(sink_f - m)
        p = (e / denom).astype(v_blk.dtype)
        return jnp.einsum('bhgqk,bkhd->bqhgd', p, v_blk)

    o = lax.map(block, (jnp.arange(nb), qb))
    o = jnp.moveaxis(o, 0, 1).reshape(B, S, nq)
    return o @ w_o


def diff_attention(h, w_qkv, q_g, k_g, lq1, lk1, lq2, lk2, subln_g, w_o, cos, sin, lambda_init):
    B, S, _ = h.shape
    nqk = DIFF_HEADS * 2 * HEAD_DIM
    qkv = h @ w_qkv
    q = qkv[..., :nqk].reshape(B, S, DIFF_HEADS, 2, HEAD_DIM)
    k = qkv[..., nqk:2 * nqk].reshape(B, S, DIFF_HEADS, 2, HEAD_DIM)
    v = qkv[..., 2 * nqk:].reshape(B, S, DIFF_HEADS, 2 * HEAD_DIM)
    q = apply_rope(rms_norm(q, q_g), cos, sin)
    k = apply_rope(rms_norm(k, k_g), cos, sin)
    f32 = jnp.float32
    lam = (jnp.exp(jnp.sum(lq1.astype(f32) * lk1.astype(f32)))
           - jnp.exp(jnp.sum(lq2.astype(f32) * lk2.astype(f32))) + lambda_init)
    nb = S // BLOCK
    qb = jnp.moveaxis(q.reshape(B, nb, BLOCK, DIFF_HEADS, 2, HEAD_DIM), 1, 0)
    scale = HEAD_DIM ** -0.5

    def block(q_blk):
        s = jnp.einsum('bqhcd,bkhcd->bhcqk', q_blk, k).astype(f32) * scale
        p = jax.nn.softmax(s, axis=-1)
        a = (p[:, :, 0] - lam * p[:, :, 1]).astype(v.dtype)
        return jnp.einsum('bhqk,bkhe->bqhe', a, v)

    o = lax.map(block, qb)
    o = jnp.moveaxis(o, 0, 1).reshape(B, S, DIFF_HEADS, 2 * HEAD_DIM)
    o = rms_norm(o, subln_g) * (1.0 - lambda_init)
    return o.reshape(B, S, D_DIFF) @ w_o


def conv_glu_ffn(h, w_gate, w_up, conv_w, conv_b, w_down):
    g = h @ w_gate
    u = h @ w_up
    gp = jnp.pad(g, ((0, 0), (1, 1), (0, 0)))
    g = gp[:, :-2] * conv_w[0] + gp[:, 1:-1] * conv_w[1] + gp[:, 2:] * conv_w[2] + conv_b
    return (jax.nn.silu(g) * u) @ w_down


def _trunk(x, c, p):
    S = x.shape[1]
    cos, sin = rope_tables(S)
    c_act = jax.nn.silu(c)
    for i in range(DEPTH):
        mod = c_act @ p["ada_w"][i] + p["ada_b"][i]
        sh1, sc1, g1, sh2, sc2, g2 = jnp.split(mod[:, None, :], 6, axis=-1)
        h = rms_norm(x, p["norm1_g"][i]) * (1 + sc1) + sh1
        kind, j = i % N_MIXERS, i // N_MIXERS
        if kind == 0:
            y = fourier_mixer(h, p["fnet_w"][j], p["fnet_b"][j])
        elif kind == 1:
            y = windowed_gqa(h, p["swa_w_qkv"][j], p["swa_q_g"][j], p["swa_k_g"][j],
                             p["swa_sink"][j], p["swa_w_o"][j], cos, sin)
        else:
            lambda_init = 0.8 - 0.6 * math.exp(-0.3 * i)
            y = diff_attention(h, p["diff_w_qkv"][j], p["diff_q_g"][j], p["diff_k_g"][j],
                               p["diff_lq1"][j], p["diff_lk1"][j], p["diff_lq2"][j], p["diff_lk2"][j],
                               p["diff_subln_g"][j], p["diff_w_o"][j], cos, sin, lambda_init)
        x = x + g1 * y
        h = rms_norm(x, p["norm2_g"][i]) * (1 + sc2) + sh2
        x = x + g2 * conv_glu_ffn(h, p["ffn_w_gate"][i], p["ffn_w_up"][i], p["ffn_conv_w"][i],
                                  p["ffn_conv_b"][i], p["ffn_w_down"][i])
    return x


def setup_inputs(seed: int = 0) -> dict:
    key = jax.random.key(seed)
    ks = jax.random.split(key, 40)
    cnt = [0]

    def nrm(shape, scale):
        k = ks[cnt[0]]
        cnt[0] += 1
        return jax.random.normal(k, shape, jnp.float32) * scale

    D = D_MODEL
    qkv_swa = (SWA_Q_HEADS + 2 * SWA_KV_HEADS) * HEAD_DIM
    qkv_diff = 3 * D_DIFF
    nq = SWA_Q_HEADS * HEAD_DIM
    return {
        "x_prompt": nrm((BATCH, SEQ, D), 1.0),
        "x_sample": nrm((DEC_BATCH, DEC_SEQ, D), 1.0),
        "c_prompt": nrm((BATCH, D), 1.0),
        "c_sample": nrm((DEC_BATCH, D), 1.0),
        "ada_w": nrm((DEPTH, D, 6 * D), 0.5 * D ** -0.5),
        "ada_b": nrm((DEPTH, 6 * D), 0.02),
        "norm1_g": 1.0 + nrm((DEPTH, D), 0.02),
        "norm2_g": 1.0 + nrm((DEPTH, D), 0.02),
        "fnet_w": nrm((N_A, D, D), D ** -0.5),
        "fnet_b": nrm((N_A, D), 0.02),
        "swa_w_qkv": nrm((N_B, D, qkv_swa), D ** -0.5),
        "swa_q_g": 1.0 + nrm((N_B, HEAD_DIM), 0.02),
        "swa_k_g": 1.0 + nrm((N_B, HEAD_DIM), 0.02),
        "swa_sink": nrm((N_B, SWA_Q_HEADS), 0.5),
        "swa_w_o": nrm((N_B, nq, D), nq ** -0.5),
        "diff_w_qkv": nrm((N_C, D, qkv_diff), D ** -0.5),
        "diff_q_g": 1.0 + nrm((N_C, HEAD_DIM), 0.02),
        "diff_k_g": 1.0 + nrm((N_C, HEAD_DIM), 0.02),
        "diff_lq1": nrm((N_C, HEAD_DIM), 0.1),
        "diff_lk1": nrm((N_C, HEAD_DIM), 0.1),
        "diff_lq2": nrm((N_C, HEAD_DIM), 0.1),
        "diff_lk2": nrm((N_C, HEAD_DIM), 0.1),
        "diff_subln_g": 1.0 + nrm((N_C, 2 * HEAD_DIM), 0.02),
        "diff_w_o": nrm((N_C, D_DIFF, D), D_DIFF ** -0.5),
        "ffn_w_gate": nrm((DEPTH, D, D_FF), D ** -0.5),
        "ffn_w_up": nrm((DEPTH, D, D_FF), D ** -0.5),
        "ffn_conv_w": nrm((DEPTH, CONV_WIDTH, D_FF), CONV_WIDTH ** -0.5),
        "ffn_conv_b": nrm((DEPTH, D_FF), 0.02),
        "ffn_w_down": nrm((DEPTH, D_FF, D), D_FF ** -0.5),
    }


def reference(x_prompt, x_sample, c_prompt, c_sample, ada_w, ada_b, norm1_g, norm2_g,
              fnet_w, fnet_b, swa_w_qkv, swa_q_g, swa_k_g, swa_sink, swa_w_o,
              diff_w_qkv, diff_q_g, diff_k_g, diff_lq1, diff_lk1, diff_lq2, diff_lk2,
              diff_subln_g, diff_w_o, ffn_w_gate, ffn_w_up, ffn_conv_w, ffn_conv_b, ffn_w_down):
    p = {
        "ada_w": ada_w, "ada_b": ada_b, "norm1_g": norm1_g, "norm2_g": norm2_g,
        "fnet_w": fnet_w, "fnet_b": fnet_b,
        "swa_w_qkv": swa_w_qkv, "swa_q_g": swa_q_g, "swa_k_g": swa_k_g,
        "swa_sink": swa_sink, "swa_w_o": swa_w_o,
        "diff_w_qkv": diff_w_qkv, "diff_q_g": diff_q_g, "diff_k_g": diff_k_g,
        "diff_lq1": diff_lq1, "diff_lk1": diff_lk1, "diff_lq2": diff_lq2, "diff_lk2": diff_lk2,
        "diff_subln_g": diff_subln_g, "diff_w_o": diff_w_o,
        "ffn_w_gate": ffn_w_gate, "ffn_w_up": ffn_w_up, "ffn_conv_w": ffn_conv_w,
        "ffn_conv_b": ffn_conv_b, "ffn_w_down": ffn_w_down,
    }
    y_prompt = _trunk(x_prompt, c_prompt, p)
    y_sample = _trunk(x_sample, c_sample, p)
    return (y_prompt, y_sample)
```

```python
import functools
import math

import numpy as np
import jax
import jax.numpy as jnp
from jax import lax
from jax.experimental import pallas as pl
from jax.experimental.pallas import tpu as pltpu

D_MODEL = 1024
DEPTH = 4
HEAD_DIM = 64
SWA_Q_HEADS = 16
SWA_KV_HEADS = 4
WINDOW = 128
DIFF_HEADS = 8
D_FF = 2816
FNET_GROUP_DIM = 256
ROPE_THETA = 10000.0
EPS = 1e-6
NEG = -1e30

LANES = 128
SUBLANES = 8
VMEM_LIMIT = 56 * 1024 * 1024
FNET_ROWS = 1024
F32 = jnp.float32
BF16 = jnp.bfloat16


def _cparams(sem):
    return pltpu.CompilerParams(dimension_semantics=sem, vmem_limit_bytes=VMEM_LIMIT)


def _const_spec(shape):
    nd = len(shape)
    return pl.BlockSpec(shape, lambda *_: (0,) * nd, pipeline_mode=pl.Buffered(1))


def _norm_mod(x, gain, scale, shift):
    ms = jnp.mean(x * x, axis=-1, keepdims=True)
    return (x * lax.rsqrt(ms + EPS)) * gain * (1.0 + scale) + shift


def _dot(a, b):
    return jnp.dot(a, b, preferred_element_type=F32)


def _dot_nt(a, b):
    return lax.dot_general(a, b, (((1,), (1,)), ((), ())), preferred_element_type=F32)


def _ada_kernel(c_ref, w_ref, b_ref, o_ref):
    c = c_ref[...]
    ca = c / (1.0 + jnp.exp(-c))
    c_hi = ca.astype(BF16)
    c_lo = (ca - c_hi.astype(F32)).astype(BF16)
    w = w_ref[0]
    w_hi = w.astype(BF16)
    w_lo = (w - w_hi.astype(F32)).astype(BF16)
    o_ref[0] = _dot(c_hi, w_hi) + _dot(c_lo, w_hi) + _dot(c_hi, w_lo) + b_ref[0]


def _ada_mod(c_all, ada_w, ada_b):
    rows = c_all.shape[0]
    cn = 1536
    return pl.pallas_call(
        _ada_kernel,
        grid=(DEPTH, 6 * D_MODEL // cn),
        in_specs=[
            pl.BlockSpec((rows, D_MODEL), lambda l, j: (0, 0)),
            pl.BlockSpec((1, D_MODEL, cn), lambda l, j: (l, 0, j)),
            pl.BlockSpec((1, 1, cn), lambda l, j: (l, 0, j)),
        ],
        out_specs=pl.BlockSpec((1, rows, cn), lambda l, j: (l, 0, j)),
        out_shape=jax.ShapeDtypeStruct((DEPTH, rows, 6 * D_MODEL), F32),
        compiler_params=_cparams(("parallel", "parallel")),
        name="ada_mod",
    )(c_all, ada_w, ada_b.reshape(DEPTH, 1, 6 * D_MODEL))


FFN_CHUNKS = ((0, 1024), (1024, 2048), (2048, 2816))


def _ffn_kernel(x_ref, xp_ref, xn_ref, mod_ref, ng_ref, wg_ref, wu_ref, cw_ref, cb_ref,
                wd_ref, o_ref, *, tm):
    i = pl.program_id(1)
    last = pl.num_programs(1) - 1
    gain = ng_ref[...]
    shift, scale, gate = mod_ref[0, 3:4, :], mod_ref[0, 4:5, :], mod_ref[0, 5:6, :]
    x = x_ref[0]
    h = _norm_mod(x, gain, scale, shift).astype(BF16)
    xh = jnp.concatenate([xp_ref[0], xn_ref[0]], axis=0)
    hh = _norm_mod(xh, gain, scale, shift).astype(BF16)
    has_prev = (i > 0).astype(F32)
    has_next = (i < last).astype(F32)
    row = lax.broadcasted_iota(jnp.int32, (tm, 1), 0)
    acc = jnp.zeros((tm, D_MODEL), F32)
    for c0, c1 in FFN_CHUNKS:
        g = _dot(h, wg_ref[:, c0:c1])
        u = _dot(h, wu_ref[:, c0:c1])
        gh = _dot(hh, wg_ref[:, c0:c1])
        g_before = gh[SUBLANES - 1:SUBLANES, :] * has_prev
        g_after = gh[SUBLANES:SUBLANES + 1, :] * has_next
        g_prev = jnp.where(row == 0, g_before, pltpu.roll(g, 1, axis=0))
        g_next = jnp.where(row == tm - 1, g_after, pltpu.roll(g, tm - 1, axis=0))
        gc = (g_prev * cw_ref[0:1, c0:c1] + g * cw_ref[1:2, c0:c1]
              + g_next * cw_ref[2:3, c0:c1] + cb_ref[:, c0:c1])
        act = (gc / (1.0 + jnp.exp(-gc))) * u
        acc = acc + _dot(act.astype(BF16), wd_ref[c0:c1, :])
    o_ref[0] = x + gate * acc


def _ffn(x, mod, ng, wg, wu, cw, cb, wd, tm=512):
    B, S, D = x.shape
    nt = S // tm
    r = tm // SUBLANES
    return pl.pallas_call(
        functools.partial(_ffn_kernel, tm=tm),
        grid=(B, nt),
        in_specs=[
            pl.BlockSpec((1, tm, D), lambda b, i: (b, i, 0)),
            pl.BlockSpec((1, SUBLANES, D), lambda b, i: (b, jnp.maximum(i * r - 1, 0), 0)),
            pl.BlockSpec((1, SUBLANES, D),
                         lambda b, i: (b, jnp.minimum((i + 1) * r, S // SUBLANES - 1), 0)),
            pl.BlockSpec((1, 6, D), lambda b, i: (b, 0, 0)),
            _const_spec((1, D)),
            _const_spec((D, D_FF)),
            _const_spec((D, D_FF)),
            _const_spec((3, D_FF)),
            _const_spec((1, D_FF)),
            _const_spec((D_FF, D)),
        ],
        out_specs=pl.BlockSpec((1, tm, D), lambda b, i: (b, i, 0)),
        out_shape=jax.ShapeDtypeStruct((B, S, D), F32),
        compiler_params=_cparams(("parallel", "parallel")),
        name="conv_glu_ffn",
    )(x, x, x, mod, ng, wg, wu, cw, cb, wd)


FNET_N1 = 128


def _fnet_consts(S):
    n1 = FNET_N1
    n2 = S // n1
    a = FNET_ROWS // n2
    jb = n2 // SUBLANES
    cc = np.arange(FNET_GROUP_DIM)
    ang = 2.0 * np.pi * ((cc[:, None] * cc[None, :]) % FNET_GROUP_DIM) / FNET_GROUP_DIM
    cs = np.concatenate([np.cos(ang), np.sin(ang)], axis=1) / math.sqrt(FNET_GROUP_DIM)
    t = np.arange(n1)
    ang1 = 2.0 * np.pi * ((t[:, None] * t[None, :]) % n1) / n1
    c1, s1 = np.cos(ang1), np.sin(ang1)
    m1 = np.block([[c1, -s1], [-s1, -c1]])
    eye8 = np.eye(SUBLANES)
    k1 = np.block([[np.kron(m1[:n1, :n1], eye8), np.kron(m1[:n1, n1:], eye8)],
                   [np.kron(m1[n1:, :n1], eye8), np.kron(m1[n1:, n1:], eye8)]])
    t2 = np.arange(n2)
    angt = 2.0 * np.pi * ((t[:, None] * t2[None, :]) % S) / S
    twr = (np.cos(angt) / math.sqrt(S)).reshape(n1, jb, SUBLANES).transpose(1, 0, 2)
    twi = (-np.sin(angt) / math.sqrt(S)).reshape(n1, jb, SUBLANES).transpose(1, 0, 2)
    twr = twr.reshape(jb, n1 * SUBLANES, 1)
    twi = twi.reshape(jb, n1 * SUBLANES, 1)
    ang2 = 2.0 * np.pi * ((t2[:, None] * t2[None, :]) % n2) / n2
    eye_a = np.eye(a)

    def expand(m):
        m4 = m.reshape(n2, 1, jb, 1, SUBLANES) * eye_a.reshape(1, a, 1, a, 1)
        return m4.reshape(n2 * a, jb * a * SUBLANES)

    k3 = np.concatenate([expand(np.cos(ang2)), expand(np.sin(ang2))], axis=1)
    return dict(
        n2=n2, a=a, jb=jb,
        cs=jnp.asarray(cs, BF16), k1=jnp.asarray(k1, BF16), k3=jnp.asarray(k3, BF16),
        twr=jnp.broadcast_to(jnp.asarray(twr, F32), (jb, n1 * SUBLANES, LANES)),
        twi=jnp.broadcast_to(jnp.asarray(twi, F32), (jb, n1 * SUBLANES, LANES)),
    )


def _fnet1_kernel(x_ref, mod_ref, ng_ref, cs_ref, k1_ref, twr_ref, twi_ref, zr_ref, zi_ref):
    rows = FNET_ROWS
    x = x_ref[0].reshape(rows, D_MODEL)
    shift, scale = mod_ref[0, 0:1, :], mod_ref[0, 1:2, :]
    h = _norm_mod(x, ng_ref[...], scale, shift).astype(BF16)
    g = FNET_GROUP_DIM
    parts = [_dot(h[:, k * g:(k + 1) * g], cs_ref[...]) for k in range(D_MODEL // g)]
    a_all = jnp.concatenate([p[:, :g] for p in parts], axis=1)
    b_all = jnp.concatenate([p[:, g:] for p in parts], axis=1)
    ab = jnp.concatenate([a_all, b_all], axis=0).astype(BF16)
    z = _dot(k1_ref[...], ab)
    zr, zi = z[:rows], z[rows:]
    reps = D_MODEL // LANES
    twr = jnp.concatenate([twr_ref[0]] * reps, axis=1)
    twi = jnp.concatenate([twi_ref[0]] * reps, axis=1)
    zr_ref[0, 0] = (zr * twr - zi * twi).astype(BF16)
    zi_ref[0, 0] = (zr * twi + zi * twr).astype(BF16)


def _fnet3_kernel(zr_ref, zi_ref, k3_ref, wo_ref, bo_ref, x_ref, mod_ref, o_ref, *, jb, a, n2):
    rows = FNET_ROWS
    zr = zr_ref[0].reshape(rows, D_MODEL)
    zi = zi_ref[0].reshape(rows, D_MODEL)
    zz = jnp.concatenate([zr, zi], axis=0)
    y = _dot(k3_ref[...], zz)
    y = _dot(y.astype(BF16), wo_ref[...]) + bo_ref[...]
    gate = mod_ref[0, 2:3, :]
    x = x_ref[0].reshape(rows, D_MODEL)
    o_ref[0] = (x + gate * y).reshape(n2, a, D_MODEL)


def _fnet(x, mod, ng, wo, bo, consts):
    B, S, D = x.shape
    n1, n2, a, jb = FNET_N1, consts["n2"], consts["a"], consts["jb"]
    rows = FNET_ROWS
    x4 = x.reshape(B, n1, n2, D)
    zshape = jax.ShapeDtypeStruct((B, jb, n1 * SUBLANES, D), BF16)
    zr, zi = pl.pallas_call(
        _fnet1_kernel,
        grid=(B, jb),
        in_specs=[
            pl.BlockSpec((1, n1, SUBLANES, D), lambda b, j: (b, 0, j, 0)),
            pl.BlockSpec((1, 6, D), lambda b, j: (b, 0, 0)),
            _const_spec((1, D)),
            _const_spec((FNET_GROUP_DIM, 2 * FNET_GROUP_DIM)),
            _const_spec((2 * rows, 2 * rows)),
            pl.BlockSpec((1, rows, LANES), lambda b, j: (j, 0, 0)),
            pl.BlockSpec((1, rows, LANES), lambda b, j: (j, 0, 0)),
        ],
        out_specs=[
            pl.BlockSpec((1, 1, rows, D), lambda b, j: (b, j, 0, 0)),
            pl.BlockSpec((1, 1, rows, D), lambda b, j: (b, j, 0, 0)),
        ],
        out_shape=[zshape, zshape],
        compiler_params=_cparams(("parallel", "parallel")),
        name="fnet_stage1",
    )(x4, mod, ng, consts["cs"], consts["k1"], consts["twr"], consts["twi"])
    xs = x.reshape(B, n2, n1, D)
    out = pl.pallas_call(
        functools.partial(_fnet3_kernel, jb=jb, a=a, n2=n2),
        grid=(B, n1 // a),
        in_specs=[
            pl.BlockSpec((1, jb, a * SUBLANES, D), lambda b, i: (b, 0, i, 0)),
            pl.BlockSpec((1, jb, a * SUBLANES, D), lambda b, i: (b, 0, i, 0)),
            _const_spec((rows, 2 * rows)),
            _const_spec((D, D)),
            _const_spec((1, D)),
            pl.BlockSpec((1, n2, a, D), lambda b, i: (b, 0, i, 0)),
            pl.BlockSpec((1, 6, D), lambda b, i: (b, 0, 0)),
        ],
        out_specs=pl.BlockSpec((1, n2, a, D), lambda b, i: (b, 0, i, 0)),
        out_shape=jax.ShapeDtypeStruct((B, n2, n1, D), F32),
        compiler_params=_cparams(("parallel", "parallel")),
        name="fnet_stage3",
    )(zr, zi, consts["k3"], wo, bo, xs, mod)
    return out.reshape(B, S, D)


NR_CHUNK = 256


def _rope_tables(S):
    inv = 1.0 / (ROPE_THETA ** (jnp.arange(0, HEAD_DIM, 2, dtype=F32) / HEAD_DIM))
    ang = jnp.arange(S, dtype=F32)[:, None] * inv[None, :]
    cos, sin = jnp.cos(ang), jnp.sin(ang)
    cos128 = jnp.concatenate([cos, cos, cos, cos], axis=1)
    sin128 = jnp.concatenate([-sin, sin, -sin, sin], axis=1)
    return cos128, sin128


def _norm_rope(y, bd, gain, cos2, sin2, low):
    ms = _dot((y * y).astype(BF16), bd)
    yn = y * lax.rsqrt(ms + EPS) * gain
    up = pltpu.roll(yn, NR_CHUNK - HEAD_DIM // 2, axis=1)
    dn = pltpu.roll(yn, HEAD_DIM // 2, axis=1)
    return yn * cos2 + jnp.where(low, up, dn) * sin2


def _qkv_common(x_ref, mod_ref, ng_ref, cos_ref, sin_ref):
    shift, scale = mod_ref[0, 0:1, :], mod_ref[0, 1:2, :]
    h = _norm_mod(x_ref[0], ng_ref[...], scale, shift).astype(BF16)
    cos2 = jnp.concatenate([cos_ref[...]] * 2, axis=1)
    sin2 = jnp.concatenate([sin_ref[...]] * 2, axis=1)
    lane = lax.broadcasted_iota(jnp.int32, (1, NR_CHUNK), 1)
    low = (lane % HEAD_DIM) < HEAD_DIM // 2
    return h, cos2, sin2, low


def _swa_qkv_kernel(x_ref, mod_ref, ng_ref, w_ref, bd_ref, gq_ref, gk_ref, cos_ref, sin_ref,
                    q_ref, kv_ref):
    h, cos2, sin2, low = _qkv_common(x_ref, mod_ref, ng_ref, cos_ref, sin_ref)
    bd = bd_ref[...]
    nq = SWA_Q_HEADS * HEAD_DIM
    for c in range(nq // NR_CHUNK):
        y = _dot(h, w_ref[:, c * NR_CHUNK:(c + 1) * NR_CHUNK])
        q_ref[0, :, c * NR_CHUNK:(c + 1) * NR_CHUNK] = _norm_rope(
            y, bd, gq_ref[...], cos2, sin2, low).astype(BF16)
    k = _norm_rope(_dot(h, w_ref[:, nq:nq + NR_CHUNK]), bd, gk_ref[...], cos2, sin2, low)
    v = _dot(h, w_ref[:, nq + NR_CHUNK:nq + 2 * NR_CHUNK])
    lane = lax.broadcasted_iota(jnp.int32, (1, LANES), 1)
    lo = lane < HEAD_DIM
    k_cols, k_swaps, v_lo, v_hi = [], [], [], []
    for c in range(2):
        kc = k[:, c * LANES:(c + 1) * LANES]
        ks = pltpu.roll(kc, HEAD_DIM, axis=1)
        vc = v[:, c * LANES:(c + 1) * LANES]
        vs = pltpu.roll(vc, HEAD_DIM, axis=1)
        k_cols += [kc, ks]
        k_swaps += [ks, kc]
        v_lo += [jnp.where(lo, vc, 0.0), jnp.where(lo, vs, 0.0)]
        v_hi += [jnp.where(lo, 0.0, vs), jnp.where(lo, 0.0, vc)]
    kv_ref[0] = jnp.concatenate(k_cols + k_swaps + v_lo + v_hi, axis=1).astype(BF16)


def _diff_qkv_kernel(x_ref, mod_ref, ng_ref, w_ref, bd_ref, gq_ref, gk_ref, cos_ref, sin_ref,
                     q_ref, k_ref, v_ref):
    h, cos2, sin2, low = _qkv_common(x_ref, mod_ref, ng_ref, cos_ref, sin_ref)
    bd = bd_ref[...]
    n = DIFF_HEADS * 2 * HEAD_DIM
    for c in range(n // NR_CHUNK):
        y = _dot(h, w_ref[:, c * NR_CHUNK:(c + 1) * NR_CHUNK])
        q_ref[0, :, c * NR_CHUNK:(c + 1) * NR_CHUNK] = _norm_rope(
            y, bd, gq_ref[...], cos2, sin2, low).astype(BF16)
    for c in range(n // NR_CHUNK):
        y = _dot(h, w_ref[:, n + c * NR_CHUNK:n + (c + 1) * NR_CHUNK])
        k_ref[0, :, c * NR_CHUNK:(c + 1) * NR_CHUNK] = _norm_rope(
            y, bd, gk_ref[...], cos2, sin2, low).astype(BF16)
    v_ref[0] = _dot(h, w_ref[:, 2 * n:3 * n]).astype(BF16)


def _seg_mean_matrix():
    idx = np.arange(NR_CHUNK) // HEAD_DIM
    return jnp.asarray((idx[:, None] == idx[None, :]).astype(np.float32) / HEAD_DIM, BF16)


def _qkv(kernel, x, mod, ng, w, gq, gk, cos128, sin128, out_widths, tm=512):
    B, S, D = x.shape
    scale = HEAD_DIM ** -0.5
    gq_t = jnp.tile(gq.astype(F32) * scale, NR_CHUNK // HEAD_DIM).reshape(1, NR_CHUNK)
    gk_t = jnp.tile(gk.astype(F32), NR_CHUNK // HEAD_DIM).reshape(1, NR_CHUNK)
    return pl.pallas_call(
        kernel,
        grid=(B, S // tm),
        in_specs=[
            pl.BlockSpec((1, tm, D), lambda b, i: (b, i, 0)),
            pl.BlockSpec((1, 6, D), lambda b, i: (b, 0, 0)),
            _const_spec((1, D)),
            _const_spec(w.shape),
            _const_spec((NR_CHUNK, NR_CHUNK)),
            _const_spec((1, NR_CHUNK)),
            _const_spec((1, NR_CHUNK)),
            pl.BlockSpec((tm, LANES), lambda b, i: (i, 0)),
            pl.BlockSpec((tm, LANES), lambda b, i: (i, 0)),
        ],
        out_specs=[pl.BlockSpec((1, tm, n), lambda b, i: (b, i, 0)) for n in out_widths],
        out_shape=[jax.ShapeDtypeStruct((B, S, n), BF16) for n in out_widths],
        compiler_params=_cparams(("parallel", "parallel")),
        name=kernel.__name__.strip("_"),
    )(x, mod, ng, w, _seg_mean_matrix(), gq_t, gk_t, cos128, sin128)


def _swa_attn_kernel(sink_ref, q_ref, kvp_ref, kvm_ref, kvn_ref, x_ref, mod_ref, wo_ref, o_ref,
                     kv_scr, o_scr, *, tq, seq):
    i = pl.program_id(1)
    w = WINDOW
    kv_scr[0:w, :] = kvp_ref[0]
    kv_scr[w:w + tq, :] = kvm_ref[0]
    kv_scr[w + tq:w + tq + w, :] = kvn_ref[0]
    lane = lax.broadcasted_iota(jnp.int32, (1, LANES), 1)
    lo = lane < HEAD_DIM
    nk = 3 * w
    row = lax.broadcasted_iota(jnp.int32, (2 * w, nk), 0) % w
    col = lax.broadcasted_iota(jnp.int32, (2 * w, nk), 1) - w
    band = jnp.abs(row - col) <= WINDOW
    top = lax.broadcasted_iota(jnp.int32, (2 * w, 1), 0) < w
    kgrp = SWA_KV_HEADS * LANES

    def block(n, carry):
        r0 = pl.multiple_of(n * w, w)
        kpos = i * tq + n * w + col
        bias = jnp.where(band & (kpos >= 0) & (kpos < seq), 0.0, NEG)
        for hk in range(SWA_KV_HEADS):
            c0 = 2 * hk * LANES
            qs = jnp.concatenate([q_ref[0, pl.ds(r0, w), c0:c0 + LANES],
                                  q_ref[0, pl.ds(r0, w), c0 + LANES:c0 + 2 * LANES]], axis=0)
            zero = jnp.zeros_like(qs)
            q_lo = jnp.where(lo, qs, zero)
            q_hi = jnp.where(lo, zero, qs)
            kc = hk * LANES
            k_lo = kv_scr[pl.ds(r0, nk), kc:kc + LANES]
            k_hi = kv_scr[pl.ds(r0, nk), kgrp + kc:kgrp + kc + LANES]
            v_lo = kv_scr[pl.ds(r0, nk), 2 * kgrp + kc:2 * kgrp + kc + LANES]
            v_hi = kv_scr[pl.ds(r0, nk), 3 * kgrp + kc:3 * kgrp + kc + LANES]
            acc = jnp.zeros((2 * w, LANES), F32)
            for half, (qm, km, vm) in enumerate(((q_lo, k_lo, v_lo), (q_hi, k_hi, v_hi))):
                s = _dot_nt(qm, km) + bias
                sk = jnp.where(top, sink_ref[4 * hk + half], sink_ref[4 * hk + 2 + half])
                m = jnp.maximum(jnp.max(s, axis=-1, keepdims=True), sk)
                e = jnp.exp(s - m)
                den = jnp.sum(e, axis=-1, keepdims=True) + jnp.exp(sk - m)
                acc = acc + _dot(e.astype(BF16), vm) * (1.0 / den)
            o_scr[pl.ds(r0, w), c0:c0 + LANES] = acc[:w].astype(BF16)
            o_scr[pl.ds(r0, w), c0 + LANES:c0 + 2 * LANES] = acc[w:].astype(BF16)
        return carry

    lax.fori_loop(0, tq // w, block, 0)
    y = _dot(o_scr[...], wo_ref[...])
    o_ref[0] = x_ref[0] + mod_ref[0, 2:3, :] * y


def _swa_attn(q, kv, x, mod, sink, wo, tq=512):
    B, S, D = x.shape
    w = WINDOW
    r = tq // w
    nkv = kv.shape[-1]
    return pl.pallas_call(
        functools.partial(_swa_attn_kernel, tq=tq, seq=S),
        grid=(B, S // tq),
        in_specs=[
            pl.BlockSpec(memory_space=pltpu.SMEM),
            pl.BlockSpec((1, tq, D), lambda b, i: (b, i, 0)),
            pl.BlockSpec((1, w, nkv), lambda b, i: (b, jnp.maximum(i * r - 1, 0), 0)),
            pl.BlockSpec((1, tq, nkv), lambda b, i: (b, i, 0)),
            pl.BlockSpec((1, w, nkv), lambda b, i: (b, jnp.minimum((i + 1) * r, S // w - 1), 0)),
            pl.BlockSpec((1, tq, D), lambda b, i: (b, i, 0)),
            pl.BlockSpec((1, 6, D), lambda b, i: (b, 0, 0)),
            _const_spec((D, D)),
        ],
        out_specs=pl.BlockSpec((1, tq, D), lambda b, i: (b, i, 0)),
        out_shape=jax.ShapeDtypeStruct((B, S, D), F32),
        scratch_shapes=[pltpu.VMEM((tq + 2 * w, nkv), BF16), pltpu.VMEM((tq, D), BF16)],
        compiler_params=_cparams(("parallel", "parallel")),
        name="swa_attention",
    )(sink, q, kv, kv, kv, x, mod, wo)


def _diff_attn_kernel(q_ref, k_ref, v_ref, lvec_ref, sg_ref, o_ref, m_scr, l_scr, acc_scr,
                      *, tq, tk, seq, lambda_init):
    lane = lax.broadcasted_iota(jnp.int32, (1, LANES), 1)
    lo = lane < HEAD_DIM
    q = q_ref[0]
    zero = jnp.zeros_like(q)
    qc = (jnp.where(lo, q, zero), jnp.where(lo, zero, q))
    m_scr[...] = jnp.full(m_scr.shape, NEG, F32)
    l_scr[...] = jnp.zeros(l_scr.shape, F32)
    acc_scr[...] = jnp.zeros(acc_scr.shape, F32)

    def step(j, carry):
        k0 = pl.multiple_of(j * tk, tk)
        kb = k_ref[0, pl.ds(k0, tk), :]
        vb = v_ref[0, pl.ds(k0, tk), :]
        for c in range(2):
            s = _dot_nt(qc[c], kb)
            m_prev = m_scr[c]
            m_new = jnp.maximum(m_prev, jnp.max(s, axis=-1, keepdims=True))
            alpha = jnp.exp(m_prev - m_new)
            p = jnp.exp(s - jnp.concatenate([m_new] * (tk // LANES), axis=1))
            l_scr[c] = alpha * l_scr[c] + jnp.sum(p, axis=-1, keepdims=True)
            acc_scr[c] = alpha * acc_scr[c] + _dot(p.astype(BF16), vb)
            m_scr[c] = m_new
        return carry

    lax.fori_loop(0, seq // tk, step, 0)
    lv = lvec_ref[...]
    lam = (jnp.exp(jnp.sum(lv[0:1] * lv[1:2], axis=-1, keepdims=True))
           - jnp.exp(jnp.sum(lv[2:3] * lv[3:4], axis=-1, keepdims=True)) + lambda_init)
    o = acc_scr[0] / l_scr[0] - lam * (acc_scr[1] / l_scr[1])
    ms = jnp.mean(o * o, axis=-1, keepdims=True)
    o_ref[0] = (o * lax.rsqrt(ms + EPS) * sg_ref[...] * (1.0 - lambda_init)).astype(BF16)


def _diff_attn(q, k, v, lvec, sg, lambda_init, tq=512, tk=512):
    B, S, D = q.shape
    return pl.pallas_call(
        functools.partial(_diff_attn_kernel, tq=tq, tk=tk, seq=S, lambda_init=lambda_init),
        grid=(B, DIFF_HEADS, S // tq),
        in_specs=[
            pl.BlockSpec((1, tq, LANES), lambda b, h, i: (b, i, h)),
            pl.BlockSpec((1, S, LANES), lambda b, h, i: (b, 0, h)),
            pl.BlockSpec((1, S, LANES), lambda b, h, i: (b, 0, h)),
            _const_spec((4, LANES)),
            _const_spec((1, LANES)),
        ],
        out_specs=pl.BlockSpec((1, tq, LANES), lambda b, h, i: (b, i, h)),
        out_shape=jax.ShapeDtypeStruct((B, S, D), BF16),
        scratch_shapes=[pltpu.VMEM((2, tq, LANES), F32), pltpu.VMEM((2, tq, LANES), F32),
                        pltpu.VMEM((2, tq, LANES), F32)],
        compiler_params=_cparams(("parallel", "parallel", "parallel")),
        name="diff_attention",
    )(q, k, v, lvec, sg)


def _proj_res_kernel(o_ref, wo_ref, x_ref, mod_ref, out_ref):
    out_ref[0] = x_ref[0] + mod_ref[0, 2:3, :] * _dot(o_ref[0], wo_ref[...])


def _proj_res(o, wo, x, mod, tm=512):
    B, S, D = x.shape
    return pl.pallas_call(
        _proj_res_kernel,
        grid=(B, S // tm),
        in_specs=[
            pl.BlockSpec((1, tm, D), lambda b, i: (b, i, 0)),
            _const_spec((D, D)),
            pl.BlockSpec((1, tm, D), lambda b, i: (b, i, 0)),
            pl.BlockSpec((1, 6, D), lambda b, i: (b, 0, 0)),
        ],
        out_specs=pl.BlockSpec((1, tm, D), lambda b, i: (b, i, 0)),
        out_shape=jax.ShapeDtypeStruct((B, S, D), F32),
        compiler_params=_cparams(("parallel", "parallel")),
        name="out_proj_residual",
    )(o, wo, x, mod)


def _trunk(x, mods, p):
    S = x.shape[1]
    cos128, sin128 = _rope_tables(S)
    fconsts = _fnet_consts(S)
    for i in range(DEPTH):
        mod = mods[i]
        ng1 = p["norm1_g"][i].reshape(1, D_MODEL)
        ng2 = p["norm2_g"][i].reshape(1, D_MODEL)
        kind, j = i % 3, i // 3
        if kind == 0:
            x = _fnet(x, mod, ng1, p["fnet_w"][j], p["fnet_b"][j].reshape(1, D_MODEL), fconsts)
        elif kind == 1:
            q, kv = _qkv(_swa_qkv_kernel, x, mod, ng1, p["swa_w_qkv"][j], p["swa_q_g"][j],
                         p["swa_k_g"][j], cos128, sin128,
                         (SWA_Q_HEADS * HEAD_DIM, 4 * SWA_KV_HEADS * LANES))
            x = _swa_attn(q, kv, x, mod, p["swa_sink"][j], p["swa_w_o"][j])
        else:
            lambda_init = 0.8 - 0.6 * math.exp(-0.3 * i)
            n = DIFF_HEADS * 2 * HEAD_DIM
            q, k, v = _qkv(_diff_qkv_kernel, x, mod, ng1, p["diff_w_qkv"][j], p["diff_q_g"][j],
                           p["diff_k_g"][j], cos128, sin128, (n, n, n))
            pad = ((0, 0), (0, LANES - HEAD_DIM))
            lvec = jnp.pad(jnp.stack([p["diff_lq1"][j], p["diff_lk1"][j],
                                      p["diff_lq2"][j], p["diff_lk2"][j]]).astype(F32), pad)
            sg = p["diff_subln_g"][j].astype(F32).reshape(1, LANES)
            o = _diff_attn(q, k, v, lvec, sg, lambda_init)
            x = _proj_res(o, p["diff_w_o"][j], x, mod)
        x = _ffn(x, mod, ng2, p["ffn_w_gate"][i], p["ffn_w_up"][i], p["ffn_conv_w"][i],
                 p["ffn_conv_b"][i].reshape(1, D_FF), p["ffn_w_down"][i])
    return x


def kernel(x_prompt, x_sample, c_prompt, c_sample, ada_w, ada_b, norm1_g, norm2_g, fnet_w, fnet_b, swa_w_qkv, swa_q_g, swa_k_g, swa_sink, swa_w_o, diff_w_qkv, diff_q_g, diff_k_g, diff_lq1, diff_lk1, diff_lq2, diff_lk2, diff_subln_g, diff_w_o, ffn_w_gate, ffn_w_up, ffn_conv_w, ffn_conv_b, ffn_w_down):
    bp, bs = c_prompt.shape[0], c_sample.shape[0]
    rows = -(-(bp + bs) // SUBLANES) * SUBLANES
    c_all = jnp.concatenate(
        [c_prompt, c_sample, jnp.zeros((rows - bp - bs, D_MODEL), F32)], axis=0)
    mods = _ada_mod(c_all, ada_w, ada_b).reshape(DEPTH, rows, 6, D_MODEL)
    p = {
        "norm1_g": norm1_g, "norm2_g": norm2_g,
        "fnet_w": fnet_w.astype(BF16), "fnet_b": fnet_b,
        "swa_w_qkv": swa_w_qkv.astype(BF16), "swa_q_g": swa_q_g, "swa_k_g": swa_k_g,
        "swa_sink": swa_sink, "swa_w_o": swa_w_o.astype(BF16),
        "diff_w_qkv": diff_w_qkv.astype(BF16), "diff_q_g": diff_q_g, "diff_k_g": diff_k_g,
        "diff_lq1": diff_lq1, "diff_lk1": diff_lk1, "diff_lq2": diff_lq2, "diff_lk2": diff_lk2,
        "diff_subln_g": diff_subln_g, "diff_w_o": diff_w_o.astype(BF16),
        "ffn_w_gate": ffn_w_gate.astype(BF16), "ffn_w_up": ffn_w_up.astype(BF16),
        "ffn_conv_w": ffn_conv_w, "ffn_conv_b": ffn_conv_b,
        "ffn_w_down": ffn_w_down.astype(BF16),
    }
    y_prompt = _trunk(x_prompt, mods[:, :bp], p)
    y_sample = _trunk(x_sample, mods[:, bp:bp + bs], p)
    return (y_prompt, y_sample)
```

```python
import functools
import math

import numpy as np
import jax
import jax.numpy as jnp
from jax import lax
from jax.experimental import pallas as pl
from jax.experimental.pallas import tpu as pltpu

D_MODEL = 1024
DEPTH = 4
HEAD_DIM = 64
SWA_Q_HEADS = 16
SWA_KV_HEADS = 4
WINDOW = 128
DIFF_HEADS = 8
D_FF = 2816
FNET_GROUP_DIM = 256
ROPE_THETA = 10000.0
EPS = 1e-6
NEG = -1e30

LANES = 128
SUBLANES = 8
VMEM_LIMIT = 56 * 1024 * 1024
FNET_ROWS = 1024
DIFF_TILE = 512
DIFF_SUB = 256
F32 = jnp.float32
BF16 = jnp.bfloat16


def _cparams(sem):
    return pltpu.CompilerParams(dimension_semantics=sem, vmem_limit_bytes=VMEM_LIMIT)


def _const_spec(shape):
    nd = len(shape)
    return pl.BlockSpec(shape, lambda *_: (0,) * nd, pipeline_mode=pl.Buffered(1))


def _norm_mod(x, gain, scale, shift):
    ms = jnp.mean(x * x, axis=-1, keepdims=True)
    return (x * lax.rsqrt(ms + EPS)) * gain * (1.0 + scale) + shift


def _dot(a, b):
    return jnp.dot(a, b, preferred_element_type=F32)


def _dot_nt(a, b):
    return lax.dot_general(a, b, (((1,), (1,)), ((), ())), preferred_element_type=F32)


def _ada_kernel(c_ref, w_ref, b_ref, o_ref):
    c = c_ref[...]
    ca = c / (1.0 + jnp.exp(-c))
    c_hi = ca.astype(BF16)
    c_lo = (ca - c_hi.astype(F32)).astype(BF16)
    w = w_ref[0]
    w_hi = w.astype(BF16)
    w_lo = (w - w_hi.astype(F32)).astype(BF16)
    o_ref[0] = _dot(c_hi, w_hi) + _dot(c_lo, w_hi) + _dot(c_hi, w_lo) + b_ref[0]


def _ada_mod(c_all, ada_w, ada_b):
    rows = c_all.shape[0]
    cn = 1536
    return pl.pallas_call(
        _ada_kernel,
        grid=(DEPTH, 6 * D_MODEL // cn),
        in_specs=[
            pl.BlockSpec((rows, D_MODEL), lambda l, j: (0, 0)),
            pl.BlockSpec((1, D_MODEL, cn), lambda l, j: (l, 0, j)),
            pl.BlockSpec((1, 1, cn), lambda l, j: (l, 0, j)),
        ],
        out_specs=pl.BlockSpec((1, rows, cn), lambda l, j: (l, 0, j)),
        out_shape=jax.ShapeDtypeStruct((DEPTH, rows, 6 * D_MODEL), F32),
        compiler_params=_cparams(("parallel", "parallel")),
        name="ada_mod",
    )(c_all, ada_w, ada_b.reshape(DEPTH, 1, 6 * D_MODEL))


FFN_CHUNKS = ((0, 1024), (1024, 2048), (2048, 2816))


def _ffn_kernel(x_ref, xp_ref, xn_ref, mod_ref, ng_ref, wg_ref, wu_ref, cw_ref, cb_ref,
                wd_ref, o_ref, *, tm):
    i = pl.program_id(1)
    last = pl.num_programs(1) - 1
    gain = ng_ref[...]
    shift, scale, gate = mod_ref[0, 3:4, :], mod_ref[0, 4:5, :], mod_ref[0, 5:6, :]
    x = x_ref[0]
    h = _norm_mod(x, gain, scale, shift).astype(BF16)
    xh = jnp.concatenate([xp_ref[0], xn_ref[0]], axis=0)
    hh = _norm_mod(xh, gain, scale, shift).astype(BF16)
    has_prev = (i > 0).astype(F32)
    has_next = (i < last).astype(F32)
    row = lax.broadcasted_iota(jnp.int32, (tm, 1), 0)
    acc = jnp.zeros((tm, D_MODEL), F32)
    for c0, c1 in FFN_CHUNKS:
        g = _dot(h, wg_ref[:, c0:c1])
        u = _dot(h, wu_ref[:, c0:c1])
        gh = _dot(hh, wg_ref[:, c0:c1])
        g_before = gh[SUBLANES - 1:SUBLANES, :] * has_prev
        g_after = gh[SUBLANES:SUBLANES + 1, :] * has_next
        g_prev = jnp.where(row == 0, g_before, pltpu.roll(g, 1, axis=0))
        g_next = jnp.where(row == tm - 1, g_after, pltpu.roll(g, tm - 1, axis=0))
        gc = (g_prev * cw_ref[0:1, c0:c1] + g * cw_ref[1:2, c0:c1]
              + g_next * cw_ref[2:3, c0:c1] + cb_ref[:, c0:c1])
        act = (gc / (1.0 + jnp.exp(-gc))) * u
        acc = acc + _dot(act.astype(BF16), wd_ref[c0:c1, :])
    o_ref[0] = x + gate * acc


def _ffn(x, mod, ng, wg, wu, cw, cb, wd, tm=512):
    B, S, D = x.shape
    nt = S // tm
    r = tm // SUBLANES
    return pl.pallas_call(
        functools.partial(_ffn_kernel, tm=tm),
        grid=(B, nt),
        in_specs=[
            pl.BlockSpec((1, tm, D), lambda b, i: (b, i, 0)),
            pl.BlockSpec((1, SUBLANES, D), lambda b, i: (b, jnp.maximum(i * r - 1, 0), 0)),
            pl.BlockSpec((1, SUBLANES, D),
                         lambda b, i: (b, jnp.minimum((i + 1) * r, S // SUBLANES - 1), 0)),
            pl.BlockSpec((1, 6, D), lambda b, i: (b, 0, 0)),
            _const_spec((1, D)),
            _const_spec((D, D_FF)),
            _const_spec((D, D_FF)),
            _const_spec((3, D_FF)),
            _const_spec((1, D_FF)),
            _const_spec((D_FF, D)),
        ],
        out_specs=pl.BlockSpec((1, tm, D), lambda b, i: (b, i, 0)),
        out_shape=jax.ShapeDtypeStruct((B, S, D), F32),
        compiler_params=_cparams(("parallel", "parallel")),
        name="conv_glu_ffn",
    )(x, x, x, mod, ng, wg, wu, cw, cb, wd)


FNET_N1 = 128


def _fnet_consts(S):
    n1 = FNET_N1
    n2 = S // n1
    a = FNET_ROWS // n2
    jb = n2 // SUBLANES
    cc = np.arange(FNET_GROUP_DIM)
    ang = 2.0 * np.pi * ((cc[:, None] * cc[None, :]) % FNET_GROUP_DIM) / FNET_GROUP_DIM
    cs = np.concatenate([np.cos(ang), np.sin(ang)], axis=1) / math.sqrt(FNET_GROUP_DIM)
    t = np.arange(n1)
    ang1 = 2.0 * np.pi * ((t[:, None] * t[None, :]) % n1) / n1
    c1, s1 = np.cos(ang1), np.sin(ang1)
    m1 = np.block([[c1, -s1], [-s1, -c1]])
    eye8 = np.eye(SUBLANES)
    k1 = np.block([[np.kron(m1[:n1, :n1], eye8), np.kron(m1[:n1, n1:], eye8)],
                   [np.kron(m1[n1:, :n1], eye8), np.kron(m1[n1:, n1:], eye8)]])
    t2 = np.arange(n2)
    angt = 2.0 * np.pi * ((t[:, None] * t2[None, :]) % S) / S
    twr = (np.cos(angt) / math.sqrt(S)).reshape(n1, jb, SUBLANES).transpose(1, 0, 2)
    twi = (-np.sin(angt) / math.sqrt(S)).reshape(n1, jb, SUBLANES).transpose(1, 0, 2)
    twr = twr.reshape(jb, n1 * SUBLANES, 1)
    twi = twi.reshape(jb, n1 * SUBLANES, 1)
    ang2 = 2.0 * np.pi * ((t2[:, None] * t2[None, :]) % n2) / n2
    eye_a = np.eye(a)

    def expand(m):
        m4 = m.reshape(n2, 1, jb, 1, SUBLANES) * eye_a.reshape(1, a, 1, a, 1)
        return m4.reshape(n2 * a, jb * a * SUBLANES)

    k3 = np.concatenate([expand(np.cos(ang2)), expand(np.sin(ang2))], axis=1)
    return dict(
        n2=n2, a=a, jb=jb,
        cs=jnp.asarray(cs, BF16), k1=jnp.asarray(k1, BF16), k3=jnp.asarray(k3, BF16),
        twr=jnp.broadcast_to(jnp.asarray(twr, F32), (jb, n1 * SUBLANES, LANES)),
        twi=jnp.broadcast_to(jnp.asarray(twi, F32), (jb, n1 * SUBLANES, LANES)),
    )


def _fnet1_kernel(x_ref, mod_ref, ng_ref, cs_ref, k1_ref, twr_ref, twi_ref, zr_ref, zi_ref):
    rows = FNET_ROWS
    x = x_ref[0].reshape(rows, D_MODEL)
    shift, scale = mod_ref[0, 0:1, :], mod_ref[0, 1:2, :]
    h = _norm_mod(x, ng_ref[...], scale, shift).astype(BF16)
    g = FNET_GROUP_DIM
    parts = [_dot(h[:, k * g:(k + 1) * g], cs_ref[...]) for k in range(D_MODEL // g)]
    a_all = jnp.concatenate([p[:, :g] for p in parts], axis=1)
    b_all = jnp.concatenate([p[:, g:] for p in parts], axis=1)
    ab = jnp.concatenate([a_all, b_all], axis=0).astype(BF16)
    z = _dot(k1_ref[...], ab)
    zr, zi = z[:rows], z[rows:]
    reps = D_MODEL // LANES
    twr = jnp.concatenate([twr_ref[0]] * reps, axis=1)
    twi = jnp.concatenate([twi_ref[0]] * reps, axis=1)
    zr_ref[0, 0] = (zr * twr - zi * twi).astype(BF16)
    zi_ref[0, 0] = (zr * twi + zi * twr).astype(BF16)


def _fnet3_kernel(zr_ref, zi_ref, k3_ref, wo_ref, bo_ref, x_ref, mod_ref, o_ref, *, jb, a, n2):
    rows = FNET_ROWS
    zr = zr_ref[0].reshape(rows, D_MODEL)
    zi = zi_ref[0].reshape(rows, D_MODEL)
    zz = jnp.concatenate([zr, zi], axis=0)
    y = _dot(k3_ref[...], zz)
    y = _dot(y.astype(BF16), wo_ref[...]) + bo_ref[...]
    gate = mod_ref[0, 2:3, :]
    x = x_ref[0].reshape(rows, D_MODEL)
    o_ref[0] = (x + gate * y).reshape(n2, a, D_MODEL)


def _fnet(x, mod, ng, wo, bo, consts):
    B, S, D = x.shape
    n1, n2, a, jb = FNET_N1, consts["n2"], consts["a"], consts["jb"]
    rows = FNET_ROWS
    x4 = x.reshape(B, n1, n2, D)
    zshape = jax.ShapeDtypeStruct((B, jb, n1 * SUBLANES, D), BF16)
    zr, zi = pl.pallas_call(
        _fnet1_kernel,
        grid=(B, jb),
        in_specs=[
            pl.BlockSpec((1, n1, SUBLANES, D), lambda b, j: (b, 0, j, 0)),
            pl.BlockSpec((1, 6, D), lambda b, j: (b, 0, 0)),
            _const_spec((1, D)),
            _const_spec((FNET_GROUP_DIM, 2 * FNET_GROUP_DIM)),
            _const_spec((2 * rows, 2 * rows)),
            pl.BlockSpec((1, rows, LANES), lambda b, j: (j, 0, 0)),
            pl.BlockSpec((1, rows, LANES), lambda b, j: (j, 0, 0)),
        ],
        out_specs=[
            pl.BlockSpec((1, 1, rows, D), lambda b, j: (b, j, 0, 0)),
            pl.BlockSpec((1, 1, rows, D), lambda b, j: (b, j, 0, 0)),
        ],
        out_shape=[zshape, zshape],
        compiler_params=_cparams(("parallel", "parallel")),
        name="fnet_stage1",
    )(x4, mod, ng, consts["cs"], consts["k1"], consts["twr"], consts["twi"])
    xs = x.reshape(B, n2, n1, D)
    out = pl.pallas_call(
        functools.partial(_fnet3_kernel, jb=jb, a=a, n2=n2),
        grid=(B, n1 // a),
        in_specs=[
            pl.BlockSpec((1, jb, a * SUBLANES, D), lambda b, i: (b, 0, i, 0)),
            pl.BlockSpec((1, jb, a * SUBLANES, D), lambda b, i: (b, 0, i, 0)),
            _const_spec((rows, 2 * rows)),
            _const_spec((D, D)),
            _const_spec((1, D)),
            pl.BlockSpec((1, n2, a, D), lambda b, i: (b, 0, i, 0)),
            pl.BlockSpec((1, 6, D), lambda b, i: (b, 0, 0)),
        ],
        out_specs=pl.BlockSpec((1, n2, a, D), lambda b, i: (b, 0, i, 0)),
        out_shape=jax.ShapeDtypeStruct((B, n2, n1, D), F32),
        compiler_params=_cparams(("parallel", "parallel")),
        name="fnet_stage3",
    )(zr, zi, consts["k3"], wo, bo, xs, mod)
    return out.reshape(B, S, D)


NR_CHUNK = 256


def _rope_tables(S):
    inv = 1.0 / (ROPE_THETA ** (jnp.arange(0, HEAD_DIM, 2, dtype=F32) / HEAD_DIM))
    ang = jnp.arange(S, dtype=F32)[:, None] * inv[None, :]
    cos, sin = jnp.cos(ang), jnp.sin(ang)
    cos128 = jnp.concatenate([cos, cos, cos, cos], axis=1)
    sin128 = jnp.concatenate([-sin, sin, -sin, sin], axis=1)
    return cos128, sin128


def _norm_rope(y, bd, gain, cos2, sin2, low):
    ms = _dot((y * y).astype(BF16), bd)
    yn = y * lax.rsqrt(ms + EPS) * gain
    up = pltpu.roll(yn, NR_CHUNK - HEAD_DIM // 2, axis=1)
    dn = pltpu.roll(yn, HEAD_DIM // 2, axis=1)
    return yn * cos2 + jnp.where(low, up, dn) * sin2


def _qkv_common(x_ref, mod_ref, ng_ref, cos_ref, sin_ref):
    shift, scale = mod_ref[0, 0:1, :], mod_ref[0, 1:2, :]
    h = _norm_mod(x_ref[0], ng_ref[...], scale, shift).astype(BF16)
    cos2 = jnp.concatenate([cos_ref[...]] * 2, axis=1)
    sin2 = jnp.concatenate([sin_ref[...]] * 2, axis=1)
    lane = lax.broadcasted_iota(jnp.int32, (1, NR_CHUNK), 1)
    low = (lane % HEAD_DIM) < HEAD_DIM // 2
    return h, cos2, sin2, low


def _swa_qkv_kernel(x_ref, mod_ref, ng_ref, w_ref, bd_ref, gq_ref, gk_ref, cos_ref, sin_ref,
                    q_ref, kv_ref):
    h, cos2, sin2, low = _qkv_common(x_ref, mod_ref, ng_ref, cos_ref, sin_ref)
    bd = bd_ref[...]
    nq = SWA_Q_HEADS * HEAD_DIM
    for c in range(nq // NR_CHUNK):
        y = _dot(h, w_ref[:, c * NR_CHUNK:(c + 1) * NR_CHUNK])
        q_ref[0, :, c * NR_CHUNK:(c + 1) * NR_CHUNK] = _norm_rope(
            y, bd, gq_ref[...], cos2, sin2, low).astype(BF16)
    k = _norm_rope(_dot(h, w_ref[:, nq:nq + NR_CHUNK]), bd, gk_ref[...], cos2, sin2, low)
    v = _dot(h, w_ref[:, nq + NR_CHUNK:nq + 2 * NR_CHUNK])
    lane = lax.broadcasted_iota(jnp.int32, (1, LANES), 1)
    lo = lane < HEAD_DIM
    k_cols, k_swaps, v_lo, v_hi = [], [], [], []
    for c in range(2):
        kc = k[:, c * LANES:(c + 1) * LANES]
        ks = pltpu.roll(kc, HEAD_DIM, axis=1)
        vc = v[:, c * LANES:(c + 1) * LANES]
        vs = pltpu.roll(vc, HEAD_DIM, axis=1)
        k_cols += [kc, ks]
        k_swaps += [ks, kc]
        v_lo += [jnp.where(lo, vc, 0.0), jnp.where(lo, vs, 0.0)]
        v_hi += [jnp.where(lo, 0.0, vs), jnp.where(lo, 0.0, vc)]
    kv_ref[0] = jnp.concatenate(k_cols + k_swaps + v_lo + v_hi, axis=1).astype(BF16)


def _diff_qkv_kernel(x_ref, mod_ref, ng_ref, wqt_ref, wk_ref, wvt_ref, bd_ref, gqt_ref, gk_ref,
                     cos_ref, sin_ref, cost_ref, sint_ref, qt_ref, k_ref, vt_ref):
    h, cos2, sin2, low = _qkv_common(x_ref, mod_ref, ng_ref, cos_ref, sin_ref)
    tm = h.shape[0]
    bd = bd_ref[...]
    n = DIFF_HEADS * 2 * HEAD_DIM
    heads = NR_CHUNK // HEAD_DIM
    half = HEAD_DIM // 2
    reps = tm // LANES
    gqt = jnp.concatenate([gqt_ref[...]] * reps, axis=1)
    cost, sint = cost_ref[...], sint_ref[...]
    for c in range(n // NR_CHUNK):
        yt = _dot_nt(wqt_ref[c * NR_CHUNK:(c + 1) * NR_CHUNK, :], h)
        y3 = yt.reshape(heads, HEAD_DIM, tm)
        ms = jnp.mean(y3 * y3, axis=1, keepdims=True)
        yn = y3 * lax.rsqrt(ms + EPS) * gqt
        partner = jnp.concatenate([yn[:, half:], yn[:, :half]], axis=1)
        qt = yn * cost + partner * sint
        qt_ref[0, 0, c * NR_CHUNK:(c + 1) * NR_CHUNK, :] = qt.reshape(NR_CHUNK, tm).astype(BF16)
    for c in range(n // NR_CHUNK):
        y = _dot(h, wk_ref[:, c * NR_CHUNK:(c + 1) * NR_CHUNK])
        k_ref[0, :, c * NR_CHUNK:(c + 1) * NR_CHUNK] = _norm_rope(
            y, bd, gk_ref[...], cos2, sin2, low).astype(BF16)
    vt_ref[0, 0] = _dot_nt(wvt_ref[...], h).astype(BF16)


def _diff_qkv(x, mod, ng, w, gq, gk, cos128, sin128, tm):
    B, S, D = x.shape
    n = DIFF_HEADS * 2 * HEAD_DIM
    nt = S // tm
    gq_s = gq.astype(F32) * (HEAD_DIM ** -0.5 * math.log2(math.e))
    gqt = jnp.broadcast_to(gq_s.reshape(HEAD_DIM, 1), (HEAD_DIM, LANES))
    gk_t = jnp.tile(gk.astype(F32), NR_CHUNK // HEAD_DIM).reshape(1, NR_CHUNK)
    cost = cos128[:, :HEAD_DIM].T
    sint = sin128[:, :HEAD_DIM].T
    wqt = w[:, :n].T
    wk = w[:, n:2 * n]
    wvt = w[:, 2 * n:].T
    tshape = jax.ShapeDtypeStruct((B, nt, n, tm), BF16)
    return pl.pallas_call(
        _diff_qkv_kernel,
        grid=(B, nt),
        in_specs=[
            pl.BlockSpec((1, tm, D), lambda b, i: (b, i, 0)),
            pl.BlockSpec((1, 6, D), lambda b, i: (b, 0, 0)),
            _const_spec((1, D)),
            _const_spec((n, D)),
            _const_spec((D, n)),
            _const_spec((n, D)),
            _const_spec((NR_CHUNK, NR_CHUNK)),
            _const_spec((HEAD_DIM, LANES)),
            _const_spec((1, NR_CHUNK)),
            pl.BlockSpec((tm, LANES), lambda b, i: (i, 0)),
            pl.BlockSpec((tm, LANES), lambda b, i: (i, 0)),
            pl.BlockSpec((HEAD_DIM, tm), lambda b, i: (0, i)),
            pl.BlockSpec((HEAD_DIM, tm), lambda b, i: (0, i)),
        ],
        out_specs=[
            pl.BlockSpec((1, 1, n, tm), lambda b, i: (b, i, 0, 0)),
            pl.BlockSpec((1, tm, n), lambda b, i: (b, i, 0)),
            pl.BlockSpec((1, 1, n, tm), lambda b, i: (b, i, 0, 0)),
        ],
        out_shape=[tshape, jax.ShapeDtypeStruct((B, S, n), BF16), tshape],
        compiler_params=_cparams(("parallel", "parallel")),
        name="diff_qkv",
    )(x, mod, ng, wqt, wk, wvt, _seg_mean_matrix(), gqt, gk_t, cos128, sin128, cost, sint)


def _seg_mean_matrix():
    idx = np.arange(NR_CHUNK) // HEAD_DIM
    return jnp.asarray((idx[:, None] == idx[None, :]).astype(np.float32) / HEAD_DIM, BF16)


def _qkv(kernel, x, mod, ng, w, gq, gk, cos128, sin128, out_widths, tm=512):
    B, S, D = x.shape
    scale = HEAD_DIM ** -0.5
    gq_t = jnp.tile(gq.astype(F32) * scale, NR_CHUNK // HEAD_DIM).reshape(1, NR_CHUNK)
    gk_t = jnp.tile(gk.astype(F32), NR_CHUNK // HEAD_DIM).reshape(1, NR_CHUNK)
    return pl.pallas_call(
        kernel,
        grid=(B, S // tm),
        in_specs=[
            pl.BlockSpec((1, tm, D), lambda b, i: (b, i, 0)),
            pl.BlockSpec((1, 6, D), lambda b, i: (b, 0, 0)),
            _const_spec((1, D)),
            _const_spec(w.shape),
            _const_spec((NR_CHUNK, NR_CHUNK)),
            _const_spec((1, NR_CHUNK)),
            _const_spec((1, NR_CHUNK)),
            pl.BlockSpec((tm, LANES), lambda b, i: (i, 0)),
            pl.BlockSpec((tm, LANES), lambda b, i: (i, 0)),
        ],
        out_specs=[pl.BlockSpec((1, tm, n), lambda b, i: (b, i, 0)) for n in out_widths],
        out_shape=[jax.ShapeDtypeStruct((B, S, n), BF16) for n in out_widths],
        compiler_params=_cparams(("parallel", "parallel")),
        name=kernel.__name__.strip("_"),
    )(x, mod, ng, w, _seg_mean_matrix(), gq_t, gk_t, cos128, sin128)


def _swa_attn_kernel(sink_ref, q_ref, kvp_ref, kvm_ref, kvn_ref, x_ref, mod_ref, wo_ref, o_ref,
                     kv_scr, o_scr, *, tq, seq):
    i = pl.program_id(1)
    w = WINDOW
    kv_scr[0:w, :] = kvp_ref[0]
    kv_scr[w:w + tq, :] = kvm_ref[0]
    kv_scr[w + tq:w + tq + w, :] = kvn_ref[0]
    lane = lax.broadcasted_iota(jnp.int32, (1, LANES), 1)
    lo = lane < HEAD_DIM
    nk = 3 * w
    row = lax.broadcasted_iota(jnp.int32, (2 * w, nk), 0) % w
    col = lax.broadcasted_iota(jnp.int32, (2 * w, nk), 1) - w
    band = jnp.abs(row - col) <= WINDOW
    top = lax.broadcasted_iota(jnp.int32, (2 * w, 1), 0) < w
    kgrp = SWA_KV_HEADS * LANES

    def block(n, carry):
        r0 = pl.multiple_of(n * w, w)
        kpos = i * tq + n * w + col
        bias = jnp.where(band & (kpos >= 0) & (kpos < seq), 0.0, NEG)
        for hk in range(SWA_KV_HEADS):
            c0 = 2 * hk * LANES
            qs = jnp.concatenate([q_ref[0, pl.ds(r0, w), c0:c0 + LANES],
                                  q_ref[0, pl.ds(r0, w), c0 + LANES:c0 + 2 * LANES]], axis=0)
            zero = jnp.zeros_like(qs)
            q_lo = jnp.where(lo, qs, zero)
            q_hi = jnp.where(lo, zero, qs)
            kc = hk * LANES
            k_lo = kv_scr[pl.ds(r0, nk), kc:kc + LANES]
            k_hi = kv_scr[pl.ds(r0, nk), kgrp + kc:kgrp + kc + LANES]
            v_lo = kv_scr[pl.ds(r0, nk), 2 * kgrp + kc:2 * kgrp + kc + LANES]
            v_hi = kv_scr[pl.ds(r0, nk), 3 * kgrp + kc:3 * kgrp + kc + LANES]
            acc = jnp.zeros((2 * w, LANES), F32)
            for half, (qm, km, vm) in enumerate(((q_lo, k_lo, v_lo), (q_hi, k_hi, v_hi))):
                s = _dot_nt(qm, km) + bias
                sk = jnp.where(top, sink_ref[4 * hk + half], sink_ref[4 * hk + 2 + half])
                m = jnp.maximum(jnp.max(s, axis=-1, keepdims=True), sk)
                e = jnp.exp(s - m)
                den = jnp.sum(e, axis=-1, keepdims=True) + jnp.exp(sk - m)
                acc = acc + _dot(e.astype(BF16), vm) * (1.0 / den)
            o_scr[pl.ds(r0, w), c0:c0 + LANES] = acc[:w].astype(BF16)
            o_scr[pl.ds(r0, w), c0 + LANES:c0 + 2 * LANES] = acc[w:].astype(BF16)
        return carry

    lax.fori_loop(0, tq // w, block, 0)
    y = _dot(o_scr[...], wo_ref[...])
    o_ref[0] = x_ref[0] + mod_ref[0, 2:3, :] * y


def _swa_attn(q, kv, x, mod, sink, wo, tq=512):
    B, S, D = x.shape
    w = WINDOW
    r = tq // w
    nkv = kv.shape[-1]
    return pl.pallas_call(
        functools.partial(_swa_attn_kernel, tq=tq, seq=S),
        grid=(B, S // tq),
        in_specs=[
            pl.BlockSpec(memory_space=pltpu.SMEM),
            pl.BlockSpec((1, tq, D), lambda b, i: (b, i, 0)),
            pl.BlockSpec((1, w, nkv), lambda b, i: (b, jnp.maximum(i * r - 1, 0), 0)),
            pl.BlockSpec((1, tq, nkv), lambda b, i: (b, i, 0)),
            pl.BlockSpec((1, w, nkv), lambda b, i: (b, jnp.minimum((i + 1) * r, S // w - 1), 0)),
            pl.BlockSpec((1, tq, D), lambda b, i: (b, i, 0)),
            pl.BlockSpec((1, 6, D), lambda b, i: (b, 0, 0)),
            _const_spec((D, D)),
        ],
        out_specs=pl.BlockSpec((1, tq, D), lambda b, i: (b, i, 0)),
        out_shape=jax.ShapeDtypeStruct((B, S, D), F32),
        scratch_shapes=[pltpu.VMEM((tq + 2 * w, nkv), BF16), pltpu.VMEM((tq, D), BF16)],
        compiler_params=_cparams(("parallel", "parallel")),
        name="swa_attention",
    )(sink, q, kv, kv, kv, x, mod, wo)


def _diff_attn_kernel(qt_ref, k_ref, vt_ref, lvec_ref, sgt_ref, o_ref, acc_scr, s_scr, p_scr,
                      *, tq, tk, seq, lambda_init):
    rowi = lax.broadcasted_iota(jnp.int32, (LANES, 1), 0)
    first = rowi < HEAD_DIM
    qt = qt_ref[0, 0]
    zero = jnp.zeros_like(qt)
    qc = (jnp.where(first, qt, zero), jnp.where(first, zero, qt))
    acc_scr[...] = jnp.zeros(acc_scr.shape, F32)
    nsub = tq // DIFF_SUB
    chains = [(c, hq * DIFF_SUB) for c in range(2) for hq in range(nsub)]
    qsub = [qc[c][:, q0:q0 + DIFF_SUB] for c, q0 in chains]
    n = seq // tk

    def scores(j, slot):
        k0 = pl.multiple_of(j * tk, tk)
        kb = k_ref[0, pl.ds(k0, tk), :]
        mcur = []
        for i, (c, q0) in enumerate(chains):
            s = _dot(kb, qsub[i])
            s_scr[slot, c, :, q0:q0 + DIFF_SUB] = s
            mcur.append(jnp.max(s, axis=0, keepdims=True))
        return tuple(mcur)

    def probs(slot, mcur, ms, ls):
        ms_new, ls_new, alphas = [], [], []
        for i, (c, q0) in enumerate(chains):
            m_new = jnp.maximum(ms[i], mcur[i])
            alpha = jnp.exp2(ms[i] - m_new)
            p = jnp.exp2(s_scr[slot, c, :, q0:q0 + DIFF_SUB] - m_new)
            ls_new.append(alpha * ls[i] + jnp.sum(p, axis=0, keepdims=True))
            p_scr[slot, c, :, q0:q0 + DIFF_SUB] = p.astype(BF16)
            ms_new.append(m_new)
            alphas.append(alpha)
        return tuple(ms_new), tuple(ls_new), tuple(alphas)

    def accumulate(j, slot, alphas):
        vb = vt_ref[0, j]
        for i, (c, q0) in enumerate(chains):
            acc_scr[c, :, q0:q0 + DIFF_SUB] = (
                alphas[i] * acc_scr[c, :, q0:q0 + DIFF_SUB]
                + _dot(vb, p_scr[slot, c, :, q0:q0 + DIFF_SUB]))

    ms = tuple(jnp.full((1, DIFF_SUB), NEG, F32) for _ in chains)
    ls = tuple(jnp.zeros((1, DIFF_SUB), F32) for _ in chains)
    mcur = scores(0, 0)
    ms, ls, alphas = probs(0, mcur, ms, ls)
    mcur = scores(1, 1)

    def pair(t, carry):
        mcur, ms, ls, alphas = carry
        for slot in range(2):
            j = 2 * t + slot
            mcur_next = scores(j, slot)
            accumulate(j - 2, slot, alphas)
            ms, ls, alphas = probs(1 - slot, mcur, ms, ls)
            mcur = mcur_next
        return mcur, ms, ls, alphas

    mcur, ms, ls, alphas = lax.fori_loop(1, n // 2, pair, (mcur, ms, ls, alphas))
    accumulate(n - 2, 0, alphas)
    ms, ls, alphas = probs(1, mcur, ms, ls)
    accumulate(n - 1, 1, alphas)
    l0 = jnp.concatenate(ls[:nsub], axis=1)
    l1 = jnp.concatenate(ls[nsub:], axis=1)
    lv = lvec_ref[...]
    lam = (jnp.exp(jnp.sum(lv[0:1] * lv[1:2], axis=-1, keepdims=True))
           - jnp.exp(jnp.sum(lv[2:3] * lv[3:4], axis=-1, keepdims=True)) + lambda_init)
    ot = acc_scr[0] / l0 - lam * (acc_scr[1] / l1)
    ms = jnp.mean(ot * ot, axis=0, keepdims=True)
    sgt = jnp.concatenate([sgt_ref[...]] * (tq // LANES), axis=1)
    ot = ot * lax.rsqrt(ms + EPS) * sgt * (1.0 - lambda_init)
    o_ref[0] = ot.T.astype(BF16)


def _diff_attn(qt, k, vt, lvec, sgt, lambda_init, tq, tk):
    B, S, D = k.shape
    nchunk = S // tk
    return pl.pallas_call(
        functools.partial(_diff_attn_kernel, tq=tq, tk=tk, seq=S, lambda_init=lambda_init),
        grid=(B, DIFF_HEADS, S // tq),
        in_specs=[
            pl.BlockSpec((1, 1, LANES, tq), lambda b, h, i: (b, i, h, 0)),
            pl.BlockSpec((1, S, LANES), lambda b, h, i: (b, 0, h)),
            pl.BlockSpec((1, nchunk, LANES, tk), lambda b, h, i: (b, 0, h, 0)),
            _const_spec((4, LANES)),
            _const_spec((LANES, LANES)),
        ],
        out_specs=pl.BlockSpec((1, tq, LANES), lambda b, h, i: (b, i, h)),
        out_shape=jax.ShapeDtypeStruct((B, S, D), BF16),
        scratch_shapes=[pltpu.VMEM((2, LANES, tq), F32), pltpu.VMEM((2, 2, tk, tq), F32),
                        pltpu.VMEM((2, 2, tk, tq), BF16)],
        compiler_params=_cparams(("parallel", "parallel", "parallel")),
        name="diff_attention",
    )(qt, k, vt, lvec, sgt)


def _proj_res_kernel(o_ref, wo_ref, x_ref, mod_ref, out_ref):
    out_ref[0] = x_ref[0] + mod_ref[0, 2:3, :] * _dot(o_ref[0], wo_ref[...])


def _proj_res(o, wo, x, mod, tm=512):
    B, S, D = x.shape
    return pl.pallas_call(
        _proj_res_kernel,
        grid=(B, S // tm),
        in_specs=[
            pl.BlockSpec((1, tm, D), lambda b, i: (b, i, 0)),
            _const_spec((D, D)),
            pl.BlockSpec((1, tm, D), lambda b, i: (b, i, 0)),
            pl.BlockSpec((1, 6, D), lambda b, i: (b, 0, 0)),
        ],
        out_specs=pl.BlockSpec((1, tm, D), lambda b, i: (b, i, 0)),
        out_shape=jax.ShapeDtypeStruct((B, S, D), F32),
        compiler_params=_cparams(("parallel", "parallel")),
        name="out_proj_residual",
    )(o, wo, x, mod)


def _trunk(x, mods, p):
    S = x.shape[1]
    cos128, sin128 = _rope_tables(S)
    fconsts = _fnet_consts(S)
    for i in range(DEPTH):
        mod = mods[i]
        ng1 = p["norm1_g"][i].reshape(1, D_MODEL)
        ng2 = p["norm2_g"][i].reshape(1, D_MODEL)
        kind, j = i % 3, i // 3
        if kind == 0:
            x = _fnet(x, mod, ng1, p["fnet_w"][j], p["fnet_b"][j].reshape(1, D_MODEL), fconsts)
        elif kind == 1:
            q, kv = _qkv(_swa_qkv_kernel, x, mod, ng1, p["swa_w_qkv"][j], p["swa_q_g"][j],
                         p["swa_k_g"][j], cos128, sin128,
                         (SWA_Q_HEADS * HEAD_DIM, 4 * SWA_KV_HEADS * LANES))
            x = _swa_attn(q, kv, x, mod, p["swa_sink"][j], p["swa_w_o"][j])
        else:
            lambda_init = 0.8 - 0.6 * math.exp(-0.3 * i)
            qt, k, vt = _diff_qkv(x, mod, ng1, p["diff_w_qkv"][j], p["diff_q_g"][j],
                                  p["diff_k_g"][j], cos128, sin128, DIFF_TILE)
            pad = ((0, 0), (0, LANES - HEAD_DIM))
            lvec = jnp.pad(jnp.stack([p["diff_lq1"][j], p["diff_lk1"][j],
                                      p["diff_lq2"][j], p["diff_lk2"][j]]).astype(F32), pad)
            sgt = jnp.broadcast_to(p["diff_subln_g"][j].astype(F32).reshape(LANES, 1),
                                   (LANES, LANES))
            o = _diff_attn(qt, k, vt, lvec, sgt, lambda_init, DIFF_TILE, DIFF_TILE)
            x = _proj_res(o, p["diff_w_o"][j], x, mod)
        x = _ffn(x, mod, ng2, p["ffn_w_gate"][i], p["ffn_w_up"][i], p["ffn_conv_w"][i],
                 p["ffn_conv_b"][i].reshape(1, D_FF), p["ffn_w_down"][i])
    return x


def kernel(x_prompt, x_sample, c_prompt, c_sample, ada_w, ada_b, norm1_g, norm2_g, fnet_w, fnet_b, swa_w_qkv, swa_q_g, swa_k_g, swa_sink, swa_w_o, diff_w_qkv, diff_q_g, diff_k_g, diff_lq1, diff_lk1, diff_lq2, diff_lk2, diff_subln_g, diff_w_o, ffn_w_gate, ffn_w_up, ffn_conv_w, ffn_conv_b, ffn_w_down):
    bp, bs = c_prompt.shape[0], c_sample.shape[0]
    rows = -(-(bp + bs) // SUBLANES) * SUBLANES
    c_all = jnp.concatenate(
        [c_prompt, c_sample, jnp.zeros((rows - bp - bs, D_MODEL), F32)], axis=0)
    mods = _ada_mod(c_all, ada_w, ada_b).reshape(DEPTH, rows, 6, D_MODEL)
    p = {
        "norm1_g": norm1_g, "norm2_g": norm2_g,
        "fnet_w": fnet_w.astype(BF16), "fnet_b": fnet_b,
        "swa_w_qkv": swa_w_qkv.astype(BF16), "swa_q_g": swa_q_g, "swa_k_g": swa_k_g,
        "swa_sink": swa_sink, "swa_w_o": swa_w_o.astype(BF16),
        "diff_w_qkv": diff_w_qkv.astype(BF16), "diff_q_g": diff_q_g, "diff_k_g": diff_k_g,
        "diff_lq1": diff_lq1, "diff_lk1": diff_lk1, "diff_lq2": diff_lq2, "diff_lk2": diff_lk2,
        "diff_subln_g": diff_subln_g, "diff_w_o": diff_w_o.astype(BF16),
        "ffn_w_gate": ffn_w_gate.astype(BF16), "ffn_w_up": ffn_w_up.astype(BF16),
        "ffn_conv_w": ffn_conv_w, "ffn_conv_b": ffn_conv_b,
        "ffn_w_down": ffn_w_down.astype(BF16),
    }
    y_prompt = _trunk(x_prompt, mods[:, :bp], p)
    y_sample = _trunk(x_sample, mods[:, bp:bp + bs], p)
    return (y_prompt, y_sample)
```

```python
import functools
import math

import numpy as np
import jax
import jax.numpy as jnp
from jax import lax
from jax.experimental import pallas as pl
from jax.experimental.pallas import tpu as pltpu

D_MODEL = 1024
DEPTH = 4
HEAD_DIM = 64
SWA_Q_HEADS = 16
SWA_KV_HEADS = 4
WINDOW = 128
DIFF_HEADS = 8
D_FF = 2816
FNET_GROUP_DIM = 256
ROPE_THETA = 10000.0
EPS = 1e-6
NEG = -1e30
LOG2E = math.log2(math.e)

LANES = 128
SUBLANES = 8
VMEM_LIMIT = 56 * 1024 * 1024
FNET_ROWS = 1024
DIFF_TILE = 512
DIFF_SUB = 256
F32 = jnp.float32
BF16 = jnp.bfloat16


def _cparams(sem):
    return pltpu.CompilerParams(dimension_semantics=sem, vmem_limit_bytes=VMEM_LIMIT)


def _const_spec(shape):
    nd = len(shape)
    return pl.BlockSpec(shape, lambda *_: (0,) * nd, pipeline_mode=pl.Buffered(1))


def _norm_mod(x, gain, scale, shift):
    ms = jnp.mean(x * x, axis=-1, keepdims=True)
    return (x * lax.rsqrt(ms + EPS)) * gain * (1.0 + scale) + shift


def _dot(a, b):
    return jnp.dot(a, b, preferred_element_type=F32)


def _dot_nt(a, b):
    return lax.dot_general(a, b, (((1,), (1,)), ((), ())), preferred_element_type=F32)


def _ada_kernel(c_ref, w_ref, b_ref, o_ref):
    c = c_ref[...]
    ca = c / (1.0 + jnp.exp(-c))
    c_hi = ca.astype(BF16)
    c_lo = (ca - c_hi.astype(F32)).astype(BF16)
    w = w_ref[0]
    w_hi = w.astype(BF16)
    w_lo = (w - w_hi.astype(F32)).astype(BF16)
    o_ref[0] = _dot(c_hi, w_hi) + _dot(c_lo, w_hi) + _dot(c_hi, w_lo) + b_ref[0]


def _ada_mod(c_all, ada_w, ada_b):
    rows = c_all.shape[0]
    cn = 1536
    return pl.pallas_call(
        _ada_kernel,
        grid=(DEPTH, 6 * D_MODEL // cn),
        in_specs=[
            pl.BlockSpec((rows, D_MODEL), lambda l, j: (0, 0)),
            pl.BlockSpec((1, D_MODEL, cn), lambda l, j: (l, 0, j)),
            pl.BlockSpec((1, 1, cn), lambda l, j: (l, 0, j)),
        ],
        out_specs=pl.BlockSpec((1, rows, cn), lambda l, j: (l, 0, j)),
        out_shape=jax.ShapeDtypeStruct((DEPTH, rows, 6 * D_MODEL), F32),
        compiler_params=_cparams(("parallel", "parallel")),
        name="ada_mod",
    )(c_all, ada_w, ada_b.reshape(DEPTH, 1, 6 * D_MODEL))


FFN_CHUNKS = ((0, 1024), (1024, 2048), (2048, 2816))


def _ffn_kernel(x_ref, xp_ref, xn_ref, mod_ref, ng_ref, wg_ref, wu_ref, cw_ref, cb_ref,
                wd_ref, o_ref, *, tm):
    i = pl.program_id(1)
    last = pl.num_programs(1) - 1
    gain = ng_ref[...]
    shift, scale, gate = mod_ref[0, 3:4, :], mod_ref[0, 4:5, :], mod_ref[0, 5:6, :]
    x = x_ref[0]
    h = _norm_mod(x, gain, scale, shift).astype(BF16)
    xh = jnp.concatenate([xp_ref[0], xn_ref[0]], axis=0)
    hh = _norm_mod(xh, gain, scale, shift).astype(BF16)
    has_prev = (i > 0).astype(F32)
    has_next = (i < last).astype(F32)
    row = lax.broadcasted_iota(jnp.int32, (tm, 1), 0)
    acc = jnp.zeros((tm, D_MODEL), F32)

    def gate_up(c0, c1):
        return (_dot(h, wg_ref[:, c0:c1]), _dot(h, wu_ref[:, c0:c1]),
                _dot(hh, wg_ref[:, c0:c1]))

    ahead = gate_up(*FFN_CHUNKS[0])
    for k, (c0, c1) in enumerate(FFN_CHUNKS):
        g, u, gh = ahead
        if k + 1 < len(FFN_CHUNKS):
            ahead = gate_up(*FFN_CHUNKS[k + 1])
        g_before = gh[SUBLANES - 1:SUBLANES, :] * has_prev
        g_after = gh[SUBLANES:SUBLANES + 1, :] * has_next
        g_prev = jnp.where(row == 0, g_before, pltpu.roll(g, 1, axis=0))
        g_next = jnp.where(row == tm - 1, g_after, pltpu.roll(g, tm - 1, axis=0))
        gc = (g_prev * cw_ref[0:1, c0:c1] + g * cw_ref[1:2, c0:c1]
              + g_next * cw_ref[2:3, c0:c1] + cb_ref[:, c0:c1])
        act = (gc / (1.0 + jnp.exp(-gc))) * u
        acc = acc + _dot(act.astype(BF16), wd_ref[c0:c1, :])
    o_ref[0] = x + gate * acc


def _ffn(x, mod, ng, wg, wu, cw, cb, wd, tm=1024):
    B, S, D = x.shape
    nt = S // tm
    r = tm // SUBLANES
    return pl.pallas_call(
        functools.partial(_ffn_kernel, tm=tm),
        grid=(B, nt),
        in_specs=[
            pl.BlockSpec((1, tm, D), lambda b, i: (b, i, 0)),
            pl.BlockSpec((1, SUBLANES, D), lambda b, i: (b, jnp.maximum(i * r - 1, 0), 0)),
            pl.BlockSpec((1, SUBLANES, D),
                         lambda b, i: (b, jnp.minimum((i + 1) * r, S // SUBLANES - 1), 0)),
            pl.BlockSpec((1, 6, D), lambda b, i: (b, 0, 0)),
            _const_spec((1, D)),
            _const_spec((D, D_FF)),
            _const_spec((D, D_FF)),
            _const_spec((3, D_FF)),
            _const_spec((1, D_FF)),
            _const_spec((D_FF, D)),
        ],
        out_specs=pl.BlockSpec((1, tm, D), lambda b, i: (b, i, 0)),
        out_shape=jax.ShapeDtypeStruct((B, S, D), F32),
        compiler_params=_cparams(("parallel", "parallel")),
        name="conv_glu_ffn",
    )(x, x, x, mod, ng, wg, wu, cw, cb, wd)


FNET_N1 = 128


def _pack_pair(hi, lo):
    h = lax.bitcast_convert_type(hi.astype(BF16).astype(F32), jnp.uint32)
    l = lax.bitcast_convert_type(lo.astype(BF16).astype(F32), jnp.uint32)
    return h | (l >> 16)


def _unpack_pair(w):
    hi = lax.bitcast_convert_type(w & jnp.uint32(0xFFFF0000), F32)
    lo = lax.bitcast_convert_type(w << 16, F32)
    return hi.astype(BF16), lo.astype(BF16)


def _fnet_consts(S):
    n1 = FNET_N1
    n2 = S // n1
    jb = n2 // SUBLANES
    cc = np.arange(FNET_GROUP_DIM)
    ang = 2.0 * np.pi * ((cc[:, None] * cc[None, :]) % FNET_GROUP_DIM) / FNET_GROUP_DIM
    cs = np.concatenate([np.cos(ang), np.sin(ang)], axis=1) / math.sqrt(FNET_GROUP_DIM)
    t = np.arange(n1)
    ang1 = 2.0 * np.pi * ((t[:, None] * t[None, :]) % n1) / n1
    c1, s1 = np.cos(ang1), np.sin(ang1)
    m1 = np.block([[c1, -s1], [-s1, -c1]])
    t2 = np.arange(n2)
    angt = 2.0 * np.pi * ((t[:, None] * t2[None, :]) % S) / S
    twr = (np.cos(angt) / math.sqrt(S)).T.reshape(jb, SUBLANES, n1, 1)
    twi = (-np.sin(angt) / math.sqrt(S)).T.reshape(jb, SUBLANES, n1, 1)
    ang2 = 2.0 * np.pi * ((t2[:, None] * t2[None, :]) % n2) / n2
    direct = n2 >= LANES
    a = FNET_ROWS // n2
    if direct:
        m3 = np.concatenate([np.cos(ang2), np.sin(ang2)], axis=1)
    else:
        eye_a = np.eye(a)

        def expand(m):
            m4 = m.reshape(n2, 1, 1, n2) * eye_a.reshape(1, a, a, 1)
            return m4.reshape(n2 * a, a * n2)

        m3 = np.concatenate([expand(np.cos(ang2)), expand(np.sin(ang2))], axis=1)
    return dict(
        n2=n2, a=a, jb=jb, direct=direct,
        cs=jnp.asarray(cs, BF16), m1=jnp.asarray(m1, BF16), m3=jnp.asarray(m3, BF16),
        twr=jnp.broadcast_to(jnp.asarray(twr, F32), (jb, SUBLANES, n1, LANES)),
        twi=jnp.broadcast_to(jnp.asarray(twi, F32), (jb, SUBLANES, n1, LANES)),
    )


def _fnet1_kernel(x_ref, mod_ref, ng_ref, cs_ref, m1_ref, twr_ref, twi_ref, z_ref, ab_scr,
                  zs_scr):
    n1 = FNET_N1
    rows = n1 * SUBLANES
    x = x_ref[0].reshape(rows, D_MODEL)
    shift, scale = mod_ref[0, 0:1, :], mod_ref[0, 1:2, :]
    h = _norm_mod(x, ng_ref[...], scale, shift).astype(BF16)
    g = FNET_GROUP_DIM
    reps = D_MODEL // LANES
    per = g // LANES
    for k in range(D_MODEL // g):
        p = _dot(h[:, k * g:(k + 1) * g], cs_ref[...])
        for c in range(per):
            ab_scr[0, k * per + c] = p[:, c * LANES:(c + 1) * LANES]
            ab_scr[1, k * per + c] = p[:, g + c * LANES:g + (c + 1) * LANES]
    for j in range(SUBLANES):
        sel = pl.ds(j, n1, stride=SUBLANES)
        ab = jnp.concatenate(
            [jnp.concatenate([ab_scr[half, c, sel, :] for c in range(reps)], axis=1)
             for half in range(2)], axis=0).astype(BF16)
        z = _dot(m1_ref[...], ab)
        zr, zi = z[:n1], z[n1:]
        twr = jnp.concatenate([twr_ref[0, j]] * reps, axis=1)
        twi = jnp.concatenate([twi_ref[0, j]] * reps, axis=1)
        packed = _pack_pair(zr * twr - zi * twi, zr * twi + zi * twr)
        for c in range(reps):
            zs_scr[c, sel, :] = packed[:, c * LANES:(c + 1) * LANES]
    z_ref[0] = jnp.concatenate([zs_scr[c] for c in range(reps)], axis=1).reshape(
        n1, SUBLANES, D_MODEL)


def _fnet3_kernel(z_ref, m3_ref, wo_ref, bo_ref, x_ref, mod_ref, o_ref, *scratch, a, n2,
                  direct):
    gate = mod_ref[0, 2:3, :]
    if direct:
        xs_scr, os_scr = scratch
        reps = D_MODEL // LANES
        x = x_ref[0].reshape(n2 * a, D_MODEL)
        for c in range(reps):
            xs_scr[c] = x[:, c * LANES:(c + 1) * LANES]
        ys = []
        for i in range(a):
            zr, zi = _unpack_pair(z_ref[0, i])
            ys.append(_dot(m3_ref[...], jnp.concatenate([zr, zi], axis=0)))
        y = jnp.concatenate(ys, axis=0)
        y = _dot(y.astype(BF16), wo_ref[...]) + bo_ref[...]
        for i in range(a):
            sel = pl.ds(i, n2, stride=a)
            xi = jnp.concatenate([xs_scr[c, sel, :] for c in range(reps)], axis=1)
            oi = xi + gate * y[i * n2:(i + 1) * n2]
            for c in range(reps):
                os_scr[c, sel, :] = oi[:, c * LANES:(c + 1) * LANES]
        o_ref[0] = jnp.concatenate([os_scr[c] for c in range(reps)], axis=1).reshape(
            n2, a, D_MODEL)
    else:
        zr, zi = _unpack_pair(z_ref[0].reshape(a * n2, D_MODEL))
        y = _dot(m3_ref[...], jnp.concatenate([zr, zi], axis=0))
        y = _dot(y.astype(BF16), wo_ref[...]) + bo_ref[...]
        x = x_ref[0].reshape(a * n2, D_MODEL)
        o_ref[0] = (x + gate * y).reshape(n2, a, D_MODEL)


def _fnet(x, mod, ng, wo, bo, consts):
    B, S, D = x.shape
    n1, n2, a, jb = FNET_N1, consts["n2"], consts["a"], consts["jb"]
    x4 = x.reshape(B, n1, n2, D)
    z = pl.pallas_call(
        _fnet1_kernel,
        grid=(B, jb),
        in_specs=[
            pl.BlockSpec((1, n1, SUBLANES, D), lambda b, j: (b, 0, j, 0)),
            pl.BlockSpec((1, 6, D), lambda b, j: (b, 0, 0)),
            _const_spec((1, D)),
            _const_spec((FNET_GROUP_DIM, 2 * FNET_GROUP_DIM)),
            _const_spec((2 * n1, 2 * n1)),
            pl.BlockSpec((1, SUBLANES, n1, LANES), lambda b, j: (j, 0, 0, 0)),
            pl.BlockSpec((1, SUBLANES, n1, LANES), lambda b, j: (j, 0, 0, 0)),
        ],
        out_specs=pl.BlockSpec((1, n1, SUBLANES, D), lambda b, j: (b, 0, j, 0)),
        out_shape=jax.ShapeDtypeStruct((B, n1, n2, D), jnp.uint32),
        scratch_shapes=[pltpu.VMEM((2, D // LANES, n1 * SUBLANES, LANES), F32),
                        pltpu.VMEM((D // LANES, n1 * SUBLANES, LANES), jnp.uint32)],
        compiler_params=_cparams(("parallel", "parallel")),
        name="fnet_stage1",
    )(x4, mod, ng, consts["cs"], consts["m1"], consts["twr"], consts["twi"])
    xs = x.reshape(B, n2, n1, D)
    out = pl.pallas_call(
        functools.partial(_fnet3_kernel, a=a, n2=n2, direct=consts["direct"]),
        grid=(B, n1 // a),
        in_specs=[
            pl.BlockSpec((1, a, n2, D), lambda b, i: (b, i, 0, 0)),
            _const_spec(consts["m3"].shape),
            _const_spec((D, D)),
            _const_spec((1, D)),
            pl.BlockSpec((1, n2, a, D), lambda b, i: (b, 0, i, 0)),
            pl.BlockSpec((1, 6, D), lambda b, i: (b, 0, 0)),
        ],
        out_specs=pl.BlockSpec((1, n2, a, D), lambda b, i: (b, 0, i, 0)),
        out_shape=jax.ShapeDtypeStruct((B, n2, n1, D), F32),
        scratch_shapes=([pltpu.VMEM((D // LANES, n2 * a, LANES), F32)] * 2
                        if consts["direct"] else []),
        compiler_params=_cparams(("parallel", "parallel")),
        name="fnet_stage3",
    )(z, consts["m3"], wo, bo, xs, mod)
    return out.reshape(B, S, D)


NR_CHUNK = 256


def _rope_tables(S):
    inv = 1.0 / (ROPE_THETA ** (jnp.arange(0, HEAD_DIM, 2, dtype=F32) / HEAD_DIM))
    ang = jnp.arange(S, dtype=F32)[:, None] * inv[None, :]
    cos, sin = jnp.cos(ang), jnp.sin(ang)
    cos128 = jnp.concatenate([cos, cos, cos, cos], axis=1)
    sin128 = jnp.concatenate([-sin, sin, -sin, sin], axis=1)
    return cos128, sin128


def _norm_rope(y, bd, gain, cos2, sin2, low):
    ms = _dot((y * y).astype(BF16), bd)
    yn = y * lax.rsqrt(ms + EPS) * gain
    up = pltpu.roll(yn, NR_CHUNK - HEAD_DIM // 2, axis=1)
    dn = pltpu.roll(yn, HEAD_DIM // 2, axis=1)
    return yn * cos2 + jnp.where(low, up, dn) * sin2


def _qkv_common(x_ref, mod_ref, ng_ref, cos_ref, sin_ref):
    shift, scale = mod_ref[0, 0:1, :], mod_ref[0, 1:2, :]
    h = _norm_mod(x_ref[0], ng_ref[...], scale, shift).astype(BF16)
    cos2 = jnp.concatenate([cos_ref[...]] * 2, axis=1)
    sin2 = jnp.concatenate([sin_ref[...]] * 2, axis=1)
    lane = lax.broadcasted_iota(jnp.int32, (1, NR_CHUNK), 1)
    low = (lane % HEAD_DIM) < HEAD_DIM // 2
    return h, cos2, sin2, low


def _swa_qkv_kernel(x_ref, mod_ref, ng_ref, w_ref, bd_ref, gq_ref, gk_ref, cos_ref, sin_ref,
                    q_ref, kv_ref):
    h, cos2, sin2, low = _qkv_common(x_ref, mod_ref, ng_ref, cos_ref, sin_ref)
    bd = bd_ref[...]
    nq = SWA_Q_HEADS * HEAD_DIM
    for c in range(nq // NR_CHUNK):
        y = _dot(h, w_ref[:, c * NR_CHUNK:(c + 1) * NR_CHUNK])
        q_ref[0, :, c * NR_CHUNK:(c + 1) * NR_CHUNK] = _norm_rope(
            y, bd, gq_ref[...], cos2, sin2, low).astype(BF16)
    k = _norm_rope(_dot(h, w_ref[:, nq:nq + NR_CHUNK]), bd, gk_ref[...], cos2, sin2, low)
    v = _dot(h, w_ref[:, nq + NR_CHUNK:nq + 2 * NR_CHUNK])
    lane = lax.broadcasted_iota(jnp.int32, (1, LANES), 1)
    lo = lane < HEAD_DIM
    k_cols, k_swaps, v_lo, v_hi = [], [], [], []
    for c in range(2):
        kc = k[:, c * LANES:(c + 1) * LANES]
        ks = pltpu.roll(kc, HEAD_DIM, axis=1)
        vc = v[:, c * LANES:(c + 1) * LANES]
        vs = pltpu.roll(vc, HEAD_DIM, axis=1)
        k_cols += [kc, ks]
        k_swaps += [ks, kc]
        v_lo += [jnp.where(lo, vc, 0.0), jnp.where(lo, vs, 0.0)]
        v_hi += [jnp.where(lo, 0.0, vs), jnp.where(lo, 0.0, vc)]
    kv_ref[0] = jnp.concatenate(k_cols + k_swaps + v_lo + v_hi, axis=1).astype(BF16)


def _diff_qkv_kernel(x_ref, mod_ref, ng_ref, wqt_ref, wk_ref, wvt_ref, bd_ref, gqt_ref, gk_ref,
                     cos_ref, sin_ref, cost_ref, sint_ref, qt_ref, k_ref, vt_ref):
    h, cos2, sin2, low = _qkv_common(x_ref, mod_ref, ng_ref, cos_ref, sin_ref)
    tm = h.shape[0]
    bd = bd_ref[...]
    n = DIFF_HEADS * 2 * HEAD_DIM
    heads = NR_CHUNK // HEAD_DIM
    half = HEAD_DIM // 2
    reps = tm // LANES
    gqt = jnp.concatenate([gqt_ref[...]] * reps, axis=1)
    cost, sint = cost_ref[...], sint_ref[...]
    for c in range(n // NR_CHUNK):
        yt = _dot_nt(wqt_ref[c * NR_CHUNK:(c + 1) * NR_CHUNK, :], h)
        y3 = yt.reshape(heads, HEAD_DIM, tm)
        ms = jnp.mean(y3 * y3, axis=1, keepdims=True)
        yn = y3 * lax.rsqrt(ms + EPS) * gqt
        partner = jnp.concatenate([yn[:, half:], yn[:, :half]], axis=1)
        qt = yn * cost + partner * sint
        qt_ref[0, 0, c * NR_CHUNK:(c + 1) * NR_CHUNK, :] = qt.reshape(NR_CHUNK, tm).astype(BF16)
    for c in range(n // NR_CHUNK):
        y = _dot(h, wk_ref[:, c * NR_CHUNK:(c + 1) * NR_CHUNK])
        k_ref[0, :, c * NR_CHUNK:(c + 1) * NR_CHUNK] = _norm_rope(
            y, bd, gk_ref[...], cos2, sin2, low).astype(BF16)
    vt_ref[0, 0] = _dot_nt(wvt_ref[...], h).astype(BF16)


def _diff_qkv(x, mod, ng, w, gq, gk, cos128, sin128, tm):
    B, S, D = x.shape
    n = DIFF_HEADS * 2 * HEAD_DIM
    nt = S // tm
    gq_s = gq.astype(F32) * (HEAD_DIM ** -0.5 * LOG2E)
    gqt = jnp.broadcast_to(gq_s.reshape(HEAD_DIM, 1), (HEAD_DIM, LANES))
    gk_t = jnp.tile(gk.astype(F32), NR_CHUNK // HEAD_DIM).reshape(1, NR_CHUNK)
    cost = cos128[:, :HEAD_DIM].T
    sint = sin128[:, :HEAD_DIM].T
    wqt = w[:, :n].T
    wk = w[:, n:2 * n]
    wvt = w[:, 2 * n:].T
    tshape = jax.ShapeDtypeStruct((B, nt, n, tm), BF16)
    return pl.pallas_call(
        _diff_qkv_kernel,
        grid=(B, nt),
        in_specs=[
            pl.BlockSpec((1, tm, D), lambda b, i: (b, i, 0)),
            pl.BlockSpec((1, 6, D), lambda b, i: (b, 0, 0)),
            _const_spec((1, D)),
            _const_spec((n, D)),
            _const_spec((D, n)),
            _const_spec((n, D)),
            _const_spec((NR_CHUNK, NR_CHUNK)),
            _const_spec((HEAD_DIM, LANES)),
            _const_spec((1, NR_CHUNK)),
            pl.BlockSpec((tm, LANES), lambda b, i: (i, 0)),
            pl.BlockSpec((tm, LANES), lambda b, i: (i, 0)),
            pl.BlockSpec((HEAD_DIM, tm), lambda b, i: (0, i)),
            pl.BlockSpec((HEAD_DIM, tm), lambda b, i: (0, i)),
        ],
        out_specs=[
            pl.BlockSpec((1, 1, n, tm), lambda b, i: (b, i, 0, 0)),
            pl.BlockSpec((1, tm, n), lambda b, i: (b, i, 0)),
            pl.BlockSpec((1, 1, n, tm), lambda b, i: (b, i, 0, 0)),
        ],
        out_shape=[tshape, jax.ShapeDtypeStruct((B, S, n), BF16), tshape],
        compiler_params=_cparams(("parallel", "parallel")),
        name="diff_qkv",
    )(x, mod, ng, wqt, wk, wvt, _seg_mean_matrix(), gqt, gk_t, cos128, sin128, cost, sint)


def _seg_mean_matrix():
    idx = np.arange(NR_CHUNK) // HEAD_DIM
    return jnp.asarray((idx[:, None] == idx[None, :]).astype(np.float32) / HEAD_DIM, BF16)


def _qkv(kernel, x, mod, ng, w, gq, gk, cos128, sin128, out_widths, tm=512):
    B, S, D = x.shape
    scale = HEAD_DIM ** -0.5 * LOG2E
    gq_t = jnp.tile(gq.astype(F32) * scale, NR_CHUNK // HEAD_DIM).reshape(1, NR_CHUNK)
    gk_t = jnp.tile(gk.astype(F32), NR_CHUNK // HEAD_DIM).reshape(1, NR_CHUNK)
    return pl.pallas_call(
        kernel,
        grid=(B, S // tm),
        in_specs=[
            pl.BlockSpec((1, tm, D), lambda b, i: (b, i, 0)),
            pl.BlockSpec((1, 6, D), lambda b, i: (b, 0, 0)),
            _const_spec((1, D)),
            _const_spec(w.shape),
            _const_spec((NR_CHUNK, NR_CHUNK)),
            _const_spec((1, NR_CHUNK)),
            _const_spec((1, NR_CHUNK)),
            pl.BlockSpec((tm, LANES), lambda b, i: (i, 0)),
            pl.BlockSpec((tm, LANES), lambda b, i: (i, 0)),
        ],
        out_specs=[pl.BlockSpec((1, tm, n), lambda b, i: (b, i, 0)) for n in out_widths],
        out_shape=[jax.ShapeDtypeStruct((B, S, n), BF16) for n in out_widths],
        compiler_params=_cparams(("parallel", "parallel")),
        name=kernel.__name__.strip("_"),
    )(x, mod, ng, w, _seg_mean_matrix(), gq_t, gk_t, cos128, sin128)


def _swa_attn_kernel(sink_ref, q_ref, kvp_ref, kvm_ref, kvn_ref, x_ref, mod_ref, wo_ref, o_ref,
                     kv_scr, o_scr, *, tq, seq):
    i = pl.program_id(1)
    w = WINDOW
    kv_scr[0:w, :] = kvp_ref[0]
    kv_scr[w:w + tq, :] = kvm_ref[0]
    kv_scr[w + tq:w + tq + w, :] = kvn_ref[0]
    lane = lax.broadcasted_iota(jnp.int32, (1, LANES), 1)
    lo = lane < HEAD_DIM
    nk = 3 * w
    row = lax.broadcasted_iota(jnp.int32, (2 * w, nk), 0) % w
    col = lax.broadcasted_iota(jnp.int32, (2 * w, nk), 1) - w
    band = jnp.abs(row - col) <= WINDOW
    top = lax.broadcasted_iota(jnp.int32, (2 * w, 1), 0) < w
    kgrp = SWA_KV_HEADS * LANES

    def block(n, carry):
        r0 = pl.multiple_of(n * w, w)
        kpos = i * tq + n * w + col
        bias = jnp.where(band & (kpos >= 0) & (kpos < seq), 0.0, NEG)
        chains = [(hk, half) for hk in range(SWA_KV_HEADS) for half in range(2)]

        def scores(hk, half):
            c0 = 2 * hk * LANES
            qs = jnp.concatenate([q_ref[0, pl.ds(r0, w), c0:c0 + LANES],
                                  q_ref[0, pl.ds(r0, w), c0 + LANES:c0 + 2 * LANES]], axis=0)
            zero = jnp.zeros_like(qs)
            qm = jnp.where(lo, zero, qs) if half else jnp.where(lo, qs, zero)
            kc = half * kgrp + hk * LANES
            return _dot_nt(qm, kv_scr[pl.ds(r0, nk), kc:kc + LANES]) + bias

        ahead = scores(*chains[0])
        acc = None
        for idx, (hk, half) in enumerate(chains):
            s = ahead
            if idx + 1 < len(chains):
                ahead = scores(*chains[idx + 1])
            vc = (2 + half) * kgrp + hk * LANES
            vm = kv_scr[pl.ds(r0, nk), vc:vc + LANES]
            sk = jnp.where(top, sink_ref[4 * hk + half], sink_ref[4 * hk + 2 + half]) * LOG2E
            m = jnp.maximum(jnp.max(s, axis=-1, keepdims=True), sk)
            e = jnp.exp2(s - m)
            den = jnp.sum(e, axis=-1, keepdims=True) + jnp.exp2(sk - m)
            part = _dot(e.astype(BF16), vm) * (1.0 / den)
            if half == 0:
                acc = part
            else:
                acc = acc + part
                c0 = 2 * hk * LANES
                o_scr[pl.ds(r0, w), c0:c0 + LANES] = acc[:w].astype(BF16)
                o_scr[pl.ds(r0, w), c0 + LANES:c0 + 2 * LANES] = acc[w:].astype(BF16)
        return carry

    lax.fori_loop(0, tq // w, block, 0)
    y = _dot(o_scr[...], wo_ref[...])
    o_ref[0] = x_ref[0] + mod_ref[0, 2:3, :] * y


def _swa_attn(q, kv, x, mod, sink, wo, tq=512):
    B, S, D = x.shape
    w = WINDOW
    r = tq // w
    nkv = kv.shape[-1]
    return pl.pallas_call(
        functools.partial(_swa_attn_kernel, tq=tq, seq=S),
        grid=(B, S // tq),
        in_specs=[
            pl.BlockSpec(memory_space=pltpu.SMEM),
            pl.BlockSpec((1, tq, D), lambda b, i: (b, i, 0)),
            pl.BlockSpec((1, w, nkv), lambda b, i: (b, jnp.maximum(i * r - 1, 0), 0)),
            pl.BlockSpec((1, tq, nkv), lambda b, i: (b, i, 0)),
            pl.BlockSpec((1, w, nkv), lambda b, i: (b, jnp.minimum((i + 1) * r, S // w - 1), 0)),
            pl.BlockSpec((1, tq, D), lambda b, i: (b, i, 0)),
            pl.BlockSpec((1, 6, D), lambda b, i: (b, 0, 0)),
            _const_spec((D, D)),
        ],
        out_specs=pl.BlockSpec((1, tq, D), lambda b, i: (b, i, 0)),
        out_shape=jax.ShapeDtypeStruct((B, S, D), F32),
        scratch_shapes=[pltpu.VMEM((tq + 2 * w, nkv), BF16), pltpu.VMEM((tq, D), BF16)],
        compiler_params=_cparams(("parallel", "parallel")),
        name="swa_attention",
    )(sink, q, kv, kv, kv, x, mod, wo)


def _diff_attn_kernel(qt_ref, k_ref, vt_ref, lvec_ref, sgt_ref, o_ref, acc_scr, s_scr, p_scr,
                      *, tq, tk, seq, lambda_init):
    rowi = lax.broadcasted_iota(jnp.int32, (LANES, 1), 0)
    first = rowi < HEAD_DIM
    qt = qt_ref[0, 0]
    zero = jnp.zeros_like(qt)
    qc = (jnp.where(first, qt, zero), jnp.where(first, zero, qt))
    acc_scr[...] = jnp.zeros(acc_scr.shape, F32)
    nsub = tq // DIFF_SUB
    chains = [(c, hq * DIFF_SUB) for c in range(2) for hq in range(nsub)]
    qsub = [qc[c][:, q0:q0 + DIFF_SUB] for c, q0 in chains]
    n = seq // tk

    def scores(j, slot):
        k0 = pl.multiple_of(j * tk, tk)
        kb = k_ref[0, pl.ds(k0, tk), :]
        mcur = []
        for i, (c, q0) in enumerate(chains):
            s = _dot(kb, qsub[i])
            s_scr[slot, c, :, q0:q0 + DIFF_SUB] = s
            mcur.append(jnp.max(s, axis=0, keepdims=True))
        return tuple(mcur)

    def probs(slot, mcur, ms, ls):
        ms_new, ls_new, alphas = [], [], []
        for i, (c, q0) in enumerate(chains):
            m_new = jnp.maximum(ms[i], mcur[i])
            alpha = jnp.exp2(ms[i] - m_new)
            p = jnp.exp2(s_scr[slot, c, :, q0:q0 + DIFF_SUB] - m_new)
            ls_new.append(alpha * ls[i] + jnp.sum(p, axis=0, keepdims=True))
            p_scr[slot, c, :, q0:q0 + DIFF_SUB] = p.astype(BF16)
            ms_new.append(m_new)
            alphas.append(alpha)
        return tuple(ms_new), tuple(ls_new), tuple(alphas)

    def accumulate(j, slot, alphas):
        vb = vt_ref[0, j]
        for i, (c, q0) in enumerate(chains):
            acc_scr[c, :, q0:q0 + DIFF_SUB] = (
                alphas[i] * acc_scr[c, :, q0:q0 + DIFF_SUB]
                + _dot(vb, p_scr[slot, c, :, q0:q0 + DIFF_SUB]))

    ms = tuple(jnp.full((1, DIFF_SUB), NEG, F32) for _ in chains)
    ls = tuple(jnp.zeros((1, DIFF_SUB), F32) for _ in chains)
    mcur = scores(0, 0)
    ms, ls, alphas = probs(0, mcur, ms, ls)
    mcur = scores(1, 1)

    def pair(t, carry):
        mcur, ms, ls, alphas = carry
        for slot in range(2):
            j = 2 * t + slot
            mcur_next = scores(j, slot)
            accumulate(j - 2, slot, alphas)
            ms, ls, alphas = probs(1 - slot, mcur, ms, ls)
            mcur = mcur_next
        return mcur, ms, ls, alphas

    mcur, ms, ls, alphas = lax.fori_loop(1, n // 2, pair, (mcur, ms, ls, alphas))
    accumulate(n - 2, 0, alphas)
    ms, ls, alphas = probs(1, mcur, ms, ls)
    accumulate(n - 1, 1, alphas)
    l0 = jnp.concatenate(ls[:nsub], axis=1)
    l1 = jnp.concatenate(ls[nsub:], axis=1)
    lv = lvec_ref[...]
    lam = (jnp.exp(jnp.sum(lv[0:1] * lv[1:2], axis=-1, keepdims=True))
           - jnp.exp(jnp.sum(lv[2:3] * lv[3:4], axis=-1, keepdims=True)) + lambda_init)
    ot = acc_scr[0] / l0 - lam * (acc_scr[1] / l1)
    ms = jnp.mean(ot * ot, axis=0, keepdims=True)
    sgt = jnp.concatenate([sgt_ref[...]] * (tq // LANES), axis=1)
    ot = ot * lax.rsqrt(ms + EPS) * sgt * (1.0 - lambda_init)
    o_ref[0] = ot.T.astype(BF16)


def _diff_attn(qt, k, vt, lvec, sgt, lambda_init, tq, tk):
    B, S, D = k.shape
    nchunk = S // tk
    return pl.pallas_call(
        functools.partial(_diff_attn_kernel, tq=tq, tk=tk, seq=S, lambda_init=lambda_init),
        grid=(B, DIFF_HEADS, S // tq),
        in_specs=[
            pl.BlockSpec((1, 1, LANES, tq), lambda b, h, i: (b, i, h, 0)),
            pl.BlockSpec((1, S, LANES), lambda b, h, i: (b, 0, h)),
            pl.BlockSpec((1, nchunk, LANES, tk), lambda b, h, i: (b, 0, h, 0)),
            _const_spec((4, LANES)),
            _const_spec((LANES, LANES)),
        ],
        out_specs=pl.BlockSpec((1, tq, LANES), lambda b, h, i: (b, i, h)),
        out_shape=jax.ShapeDtypeStruct((B, S, D), BF16),
        scratch_shapes=[pltpu.VMEM((2, LANES, tq), F32), pltpu.VMEM((2, 2, tk, tq), F32),
                        pltpu.VMEM((2, 2, tk, tq), BF16)],
        compiler_params=_cparams(("parallel", "parallel", "parallel")),
        name="diff_attention",
    )(qt, k, vt, lvec, sgt)


def _proj_res_kernel(o_ref, wo_ref, x_ref, mod_ref, out_ref):
    out_ref[0] = x_ref[0] + mod_ref[0, 2:3, :] * _dot(o_ref[0], wo_ref[...])


def _proj_res(o, wo, x, mod, tm=512):
    B, S, D = x.shape
    return pl.pallas_call(
        _proj_res_kernel,
        grid=(B, S // tm),
        in_specs=[
            pl.BlockSpec((1, tm, D), lambda b, i: (b, i, 0)),
            _const_spec((D, D)),
            pl.BlockSpec((1, tm, D), lambda b, i: (b, i, 0)),
            pl.BlockSpec((1, 6, D), lambda b, i: (b, 0, 0)),
        ],
        out_specs=pl.BlockSpec((1, tm, D), lambda b, i: (b, i, 0)),
        out_shape=jax.ShapeDtypeStruct((B, S, D), F32),
        compiler_params=_cparams(("parallel", "parallel")),
        name="out_proj_residual",
    )(o, wo, x, mod)


def _trunk(x, mods, p):
    S = x.shape[1]
    cos128, sin128 = _rope_tables(S)
    fconsts = _fnet_consts(S)
    for i in range(DEPTH):
        mod = mods[i]
        ng1 = p["norm1_g"][i].reshape(1, D_MODEL)
        ng2 = p["norm2_g"][i].reshape(1, D_MODEL)
        kind, j = i % 3, i // 3
        if kind == 0:
            x = _fnet(x, mod, ng1, p["fnet_w"][j], p["fnet_b"][j].reshape(1, D_MODEL), fconsts)
        elif kind == 1:
            q, kv = _qkv(_swa_qkv_kernel, x, mod, ng1, p["swa_w_qkv"][j], p["swa_q_g"][j],
                         p["swa_k_g"][j], cos128, sin128,
                         (SWA_Q_HEADS * HEAD_DIM, 4 * SWA_KV_HEADS * LANES))
            x = _swa_attn(q, kv, x, mod, p["swa_sink"][j], p["swa_w_o"][j])
        else:
            lambda_init = 0.8 - 0.6 * math.exp(-0.3 * i)
            qt, k, vt = _diff_qkv(x, mod, ng1, p["diff_w_qkv"][j], p["diff_q_g"][j],
                                  p["diff_k_g"][j], cos128, sin128, DIFF_TILE)
            pad = ((0, 0), (0, LANES - HEAD_DIM))
            lvec = jnp.pad(jnp.stack([p["diff_lq1"][j], p["diff_lk1"][j],
                                      p["diff_lq2"][j], p["diff_lk2"][j]]).astype(F32), pad)
            sgt = jnp.broadcast_to(p["diff_subln_g"][j].astype(F32).reshape(LANES, 1),
                                   (LANES, LANES))
            o = _diff_attn(qt, k, vt, lvec, sgt, lambda_init, DIFF_TILE, DIFF_TILE)
            x = _proj_res(o, p["diff_w_o"][j], x, mod)
        x = _ffn(x, mod, ng2, p["ffn_w_gate"][i], p["ffn_w_up"][i], p["ffn_conv_w"][i],
                 p["ffn_conv_b"][i].reshape(1, D_FF), p["ffn_w_down"][i])
    return x


def kernel(x_prompt, x_sample, c_prompt, c_sample, ada_w, ada_b, norm1_g, norm2_g, fnet_w, fnet_b, swa_w_qkv, swa_q_g, swa_k_g, swa_sink, swa_w_o, diff_w_qkv, diff_q_g, diff_k_g, diff_lq1, diff_lk1, diff_lq2, diff_lk2, diff_subln_g, diff_w_o, ffn_w_gate, ffn_w_up, ffn_conv_w, ffn_conv_b, ffn_w_down):
    bp, bs = c_prompt.shape[0], c_sample.shape[0]
    rows = -(-(bp + bs) // SUBLANES) * SUBLANES
    c_all = jnp.concatenate(
        [c_prompt, c_sample, jnp.zeros((rows - bp - bs, D_MODEL), F32)], axis=0)
    mods = _ada_mod(c_all, ada_w, ada_b).reshape(DEPTH, rows, 6, D_MODEL)
    p = {
        "norm1_g": norm1_g, "norm2_g": norm2_g,
        "fnet_w": fnet_w.astype(BF16), "fnet_b": fnet_b,
        "swa_w_qkv": swa_w_qkv.astype(BF16), "swa_q_g": swa_q_g, "swa_k_g": swa_k_g,
        "swa_sink": swa_sink, "swa_w_o": swa_w_o.astype(BF16),
        "diff_w_qkv": diff_w_qkv.astype(BF16), "diff_q_g": diff_q_g, "diff_k_g": diff_k_g,
        "diff_lq1": diff_lq1, "diff_lk1": diff_lk1, "diff_lq2": diff_lq2, "diff_lk2": diff_lk2,
        "diff_subln_g": diff_subln_g, "diff_w_o": diff_w_o.astype(BF16),
        "ffn_w_gate": ffn_w_gate.astype(BF16), "ffn_w_up": ffn_w_up.astype(BF16),
        "ffn_conv_w": ffn_conv_w, "ffn_conv_b": ffn_conv_b,
        "ffn_w_down": ffn_w_down.astype(BF16),
    }
    y_prompt = _trunk(x_prompt, mods[:, :bp], p)
    y_sample = _trunk(x_sample, mods[:, bp:bp + bs], p)
    return (y_prompt, y_sample)
```

```python
import functools
import math

import numpy as np
import jax
import jax.numpy as jnp
from jax import lax
from jax.experimental import pallas as pl
from jax.experimental.pallas import tpu as pltpu

D_MODEL = 1024
DEPTH = 4
HEAD_DIM = 64
SWA_Q_HEADS = 16
SWA_KV_HEADS = 4
WINDOW = 128
DIFF_HEADS = 8
D_FF = 2816
FNET_GROUP_DIM = 256
ROPE_THETA = 10000.0
EPS = 1e-6
NEG = -1e30
LOG2E = math.log2(math.e)

LANES = 128
SUBLANES = 8
VMEM_LIMIT = 56 * 1024 * 1024
FNET_ROWS = 1024
DIFF_TILE = 512
DIFF_SUB = 256
F32 = jnp.float32
BF16 = jnp.bfloat16


def _cparams(sem):
    return pltpu.CompilerParams(dimension_semantics=sem, vmem_limit_bytes=VMEM_LIMIT)


def _const_spec(shape):
    nd = len(shape)
    return pl.BlockSpec(shape, lambda *_: (0,) * nd, pipeline_mode=pl.Buffered(1))


def _norm_mod(x, gain, scale, shift):
    ms = jnp.mean(x * x, axis=-1, keepdims=True)
    return (x * lax.rsqrt(ms + EPS)) * gain * (1.0 + scale) + shift


def _dot(a, b):
    return jnp.dot(a, b, preferred_element_type=F32)


def _dot_nt(a, b):
    return lax.dot_general(a, b, (((1,), (1,)), ((), ())), preferred_element_type=F32)


def _ada_kernel(c_ref, w_ref, b_ref, o_ref):
    c = c_ref[...]
    ca = c / (1.0 + jnp.exp(-c))
    c_hi = ca.astype(BF16)
    c_lo = (ca - c_hi.astype(F32)).astype(BF16)
    w = w_ref[0]
    w_hi = w.astype(BF16)
    w_lo = (w - w_hi.astype(F32)).astype(BF16)
    o_ref[0] = _dot(c_hi, w_hi) + _dot(c_lo, w_hi) + _dot(c_hi, w_lo) + b_ref[0]


def _ada_mod(c_all, ada_w, ada_b):
    rows = c_all.shape[0]
    cn = 1536
    return pl.pallas_call(
        _ada_kernel,
        grid=(DEPTH, 6 * D_MODEL // cn),
        in_specs=[
            pl.BlockSpec((rows, D_MODEL), lambda l, j: (0, 0)),
            pl.BlockSpec((1, D_MODEL, cn), lambda l, j: (l, 0, j)),
            pl.BlockSpec((1, 1, cn), lambda l, j: (l, 0, j)),
        ],
        out_specs=pl.BlockSpec((1, rows, cn), lambda l, j: (l, 0, j)),
        out_shape=jax.ShapeDtypeStruct((DEPTH, rows, 6 * D_MODEL), F32),
        compiler_params=_cparams(("parallel", "parallel")),
        name="ada_mod",
    )(c_all, ada_w, ada_b.reshape(DEPTH, 1, 6 * D_MODEL))


FFN_CHUNKS = ((0, 1024), (1024, 2048), (2048, 2816))


def _ffn_kernel(x_ref, xp_ref, xn_ref, mod_ref, ng_ref, wg_ref, wu_ref, cw_ref, cb_ref,
                wd_ref, o_ref, *, tm):
    i = pl.program_id(1)
    last = pl.num_programs(1) - 1
    gain = ng_ref[...]
    shift, scale, gate = mod_ref[0, 3:4, :], mod_ref[0, 4:5, :], mod_ref[0, 5:6, :]
    x = x_ref[0]
    h = _norm_mod(x, gain, scale, shift).astype(BF16)
    xh = jnp.concatenate([xp_ref[0], xn_ref[0]], axis=0)
    hh = _norm_mod(xh, gain, scale, shift).astype(BF16)
    has_prev = (i > 0).astype(F32)
    has_next = (i < last).astype(F32)
    row = lax.broadcasted_iota(jnp.int32, (tm, 1), 0)
    acc = jnp.zeros((tm, D_MODEL), F32)

    def gate_up(c0, c1):
        return (_dot(h, wg_ref[:, c0:c1]), _dot(h, wu_ref[:, c0:c1]),
                _dot(hh, wg_ref[:, c0:c1]))

    ahead = gate_up(*FFN_CHUNKS[0])
    for k, (c0, c1) in enumerate(FFN_CHUNKS):
        g, u, gh = ahead
        if k + 1 < len(FFN_CHUNKS):
            ahead = gate_up(*FFN_CHUNKS[k + 1])
        g_before = gh[SUBLANES - 1:SUBLANES, :] * has_prev
        g_after = gh[SUBLANES:SUBLANES + 1, :] * has_next
        g_prev = jnp.where(row == 0, g_before, pltpu.roll(g, 1, axis=0))
        g_next = jnp.where(row == tm - 1, g_after, pltpu.roll(g, tm - 1, axis=0))
        gc = (g_prev * cw_ref[0:1, c0:c1] + g * cw_ref[1:2, c0:c1]
              + g_next * cw_ref[2:3, c0:c1] + cb_ref[:, c0:c1])
        act = (gc / (1.0 + jnp.exp(-gc))) * u
        acc = acc + _dot(act.astype(BF16), wd_ref[c0:c1, :])
    o_ref[0] = x + gate * acc


def _ffn(x, mod, ng, wg, wu, cw, cb, wd, tm=1024):
    B, S, D = x.shape
    nt = S // tm
    r = tm // SUBLANES
    return pl.pallas_call(
        functools.partial(_ffn_kernel, tm=tm),
        grid=(B, nt),
        in_specs=[
            pl.BlockSpec((1, tm, D), lambda b, i: (b, i, 0)),
            pl.BlockSpec((1, SUBLANES, D), lambda b, i: (b, jnp.maximum(i * r - 1, 0), 0)),
            pl.BlockSpec((1, SUBLANES, D),
                         lambda b, i: (b, jnp.minimum((i + 1) * r, S // SUBLANES - 1), 0)),
            pl.BlockSpec((1, 6, D), lambda b, i: (b, 0, 0)),
            _const_spec((1, D)),
            _const_spec((D, D_FF)),
            _const_spec((D, D_FF)),
            _const_spec((3, D_FF)),
            _const_spec((1, D_FF)),
            _const_spec((D_FF, D)),
        ],
        out_specs=pl.BlockSpec((1, tm, D), lambda b, i: (b, i, 0)),
        out_shape=jax.ShapeDtypeStruct((B, S, D), F32),
        compiler_params=_cparams(("parallel", "parallel")),
        name="conv_glu_ffn",
    )(x, x, x, mod, ng, wg, wu, cw, cb, wd)


FNET_N1 = 128


def _pack_pair(hi, lo):
    h = lax.bitcast_convert_type(hi.astype(BF16).astype(F32), jnp.uint32)
    l = lax.bitcast_convert_type(lo.astype(BF16).astype(F32), jnp.uint32)
    return h | (l >> 16)


def _unpack_pair(w):
    hi = lax.bitcast_convert_type(w & jnp.uint32(0xFFFF0000), F32)
    lo = lax.bitcast_convert_type(w << 16, F32)
    return hi.astype(BF16), lo.astype(BF16)


def _fnet_consts(S):
    n1 = FNET_N1
    n2 = S // n1
    jb = n2 // SUBLANES
    cc = np.arange(FNET_GROUP_DIM)
    ang = 2.0 * np.pi * ((cc[:, None] * cc[None, :]) % FNET_GROUP_DIM) / FNET_GROUP_DIM
    cs = np.concatenate([np.cos(ang), np.sin(ang)], axis=1) / math.sqrt(FNET_GROUP_DIM)
    t = np.arange(n1)
    ang1 = 2.0 * np.pi * ((t[:, None] * t[None, :]) % n1) / n1
    c1, s1 = np.cos(ang1), np.sin(ang1)
    m1 = np.block([[c1, -s1], [-s1, -c1]])
    t2 = np.arange(n2)
    angt = 2.0 * np.pi * ((t[:, None] * t2[None, :]) % S) / S
    twr = (np.cos(angt) / math.sqrt(S)).T.reshape(jb, SUBLANES, n1, 1)
    twi = (-np.sin(angt) / math.sqrt(S)).T.reshape(jb, SUBLANES, n1, 1)
    ang2 = 2.0 * np.pi * ((t2[:, None] * t2[None, :]) % n2) / n2
    direct = n2 >= LANES
    a = FNET_ROWS // n2
    if direct:
        m3 = np.concatenate([np.cos(ang2), np.sin(ang2)], axis=1)
    else:
        eye_a = np.eye(SUBLANES)

        def expand(m):
            m4 = m.reshape(n2, 1, 1, n2) * eye_a.reshape(1, SUBLANES, SUBLANES, 1)
            return m4.reshape(n2 * SUBLANES, SUBLANES * n2)

        m3 = np.concatenate([expand(np.cos(ang2)), expand(np.sin(ang2))], axis=1)
    return dict(
        n2=n2, a=a, jb=jb, direct=direct,
        cs=jnp.asarray(cs, BF16), m1=jnp.asarray(m1, BF16), m3=jnp.asarray(m3, BF16),
        twr=jnp.broadcast_to(jnp.asarray(twr, F32), (jb, SUBLANES, n1, LANES)),
        twi=jnp.broadcast_to(jnp.asarray(twi, F32), (jb, SUBLANES, n1, LANES)),
    )


def _fnet1_kernel(x_ref, mod_ref, ng_ref, cs_ref, m1_ref, twr_ref, twi_ref, z_ref, ab_scr,
                  zs_scr):
    n1 = FNET_N1
    rows = n1 * SUBLANES
    x = x_ref[0].reshape(rows, D_MODEL)
    shift, scale = mod_ref[0, 0:1, :], mod_ref[0, 1:2, :]
    h = _norm_mod(x, ng_ref[...], scale, shift).astype(BF16)
    g = FNET_GROUP_DIM
    reps = D_MODEL // LANES
    per = g // LANES
    for k in range(D_MODEL // g):
        p = _dot(h[:, k * g:(k + 1) * g], cs_ref[...])
        for c in range(per):
            ab_scr[0, k * per + c] = p[:, c * LANES:(c + 1) * LANES]
            ab_scr[1, k * per + c] = p[:, g + c * LANES:g + (c + 1) * LANES]
    for j in range(SUBLANES):
        sel = pl.ds(j, n1, stride=SUBLANES)
        ab = jnp.concatenate(
            [jnp.concatenate([ab_scr[half, c, sel, :] for c in range(reps)], axis=1)
             for half in range(2)], axis=0).astype(BF16)
        z = _dot(m1_ref[...], ab)
        zr, zi = z[:n1], z[n1:]
        twr = jnp.concatenate([twr_ref[0, j]] * reps, axis=1)
        twi = jnp.concatenate([twi_ref[0, j]] * reps, axis=1)
        packed = _pack_pair(zr * twr - zi * twi, zr * twi + zi * twr)
        for c in range(reps):
            zs_scr[c, sel, :] = packed[:, c * LANES:(c + 1) * LANES]
    z_ref[0] = jnp.concatenate([zs_scr[c] for c in range(reps)], axis=1).reshape(
        n1, SUBLANES, D_MODEL)


def _fnet3_kernel(z_ref, m3_ref, wo_ref, bo_ref, x_ref, mod_ref, o_ref, *scratch, a, n2,
                  direct):
    gate = mod_ref[0, 2:3, :]
    if direct:
        xs_scr, os_scr = scratch
        reps = D_MODEL // LANES
        x = x_ref[0].reshape(n2 * a, D_MODEL)
        for c in range(reps):
            xs_scr[c] = x[:, c * LANES:(c + 1) * LANES]
        ys = []
        for i in range(a):
            zr, zi = _unpack_pair(z_ref[0, i])
            ys.append(_dot(m3_ref[...], jnp.concatenate([zr, zi], axis=0)))
        y = jnp.concatenate(ys, axis=0)
        y = _dot(y.astype(BF16), wo_ref[...]) + bo_ref[...]
        for i in range(a):
            sel = pl.ds(i, n2, stride=a)
            xi = jnp.concatenate([xs_scr[c, sel, :] for c in range(reps)], axis=1)
            oi = xi + gate * y[i * n2:(i + 1) * n2]
            for c in range(reps):
                os_scr[c, sel, :] = oi[:, c * LANES:(c + 1) * LANES]
        o_ref[0] = jnp.concatenate([os_scr[c] for c in range(reps)], axis=1).reshape(
            n2, a, D_MODEL)
    else:
        g8 = SUBLANES
        rows = g8 * n2
        ys = []
        for i in range(a // g8):
            zr, zi = _unpack_pair(z_ref[0, i * g8:(i + 1) * g8].reshape(rows, D_MODEL))
            ys.append(_dot(m3_ref[...], jnp.concatenate([zr, zi], axis=0)))
        y = _dot(jnp.concatenate(ys, axis=0).astype(BF16), wo_ref[...]) + bo_ref[...]
        for i in range(a // g8):
            x = x_ref[0, :, i * g8:(i + 1) * g8, :].reshape(rows, D_MODEL)
            o_ref[0, :, i * g8:(i + 1) * g8, :] = (
                x + gate * y[i * rows:(i + 1) * rows]).reshape(n2, g8, D_MODEL)


def _fnet(x, mod, ng, wo, bo, consts):
    B, S, D = x.shape
    n1, n2, a, jb = FNET_N1, consts["n2"], consts["a"], consts["jb"]
    x4 = x.reshape(B, n1, n2, D)
    z = pl.pallas_call(
        _fnet1_kernel,
        grid=(B, jb),
        in_specs=[
            pl.BlockSpec((1, n1, SUBLANES, D), lambda b, j: (b, 0, j, 0)),
            pl.BlockSpec((1, 6, D), lambda b, j: (b, 0, 0)),
            _const_spec((1, D)),
            _const_spec((FNET_GROUP_DIM, 2 * FNET_GROUP_DIM)),
            _const_spec((2 * n1, 2 * n1)),
            pl.BlockSpec((1, SUBLANES, n1, LANES), lambda b, j: (j, 0, 0, 0)),
            pl.BlockSpec((1, SUBLANES, n1, LANES), lambda b, j: (j, 0, 0, 0)),
        ],
        out_specs=pl.BlockSpec((1, n1, SUBLANES, D), lambda b, j: (b, 0, j, 0)),
        out_shape=jax.ShapeDtypeStruct((B, n1, n2, D), jnp.uint32),
        scratch_shapes=[pltpu.VMEM((2, D // LANES, n1 * SUBLANES, LANES), F32),
                        pltpu.VMEM((D // LANES, n1 * SUBLANES, LANES), jnp.uint32)],
        compiler_params=_cparams(("parallel", "parallel")),
        name="fnet_stage1",
    )(x4, mod, ng, consts["cs"], consts["m1"], consts["twr"], consts["twi"])
    xs = x.reshape(B, n2, n1, D)
    out = pl.pallas_call(
        functools.partial(_fnet3_kernel, a=a, n2=n2, direct=consts["direct"]),
        grid=(B, n1 // a),
        in_specs=[
            pl.BlockSpec((1, a, n2, D), lambda b, i: (b, i, 0, 0)),
            _const_spec(consts["m3"].shape),
            _const_spec((D, D)),
            _const_spec((1, D)),
            pl.BlockSpec((1, n2, a, D), lambda b, i: (b, 0, i, 0)),
            pl.BlockSpec((1, 6, D), lambda b, i: (b, 0, 0)),
        ],
        out_specs=pl.BlockSpec((1, n2, a, D), lambda b, i: (b, 0, i, 0)),
        out_shape=jax.ShapeDtypeStruct((B, n2, n1, D), F32),
        scratch_shapes=([pltpu.VMEM((D // LANES, n2 * a, LANES), F32)] * 2
                        if consts["direct"] else []),
        compiler_params=_cparams(("parallel", "parallel")),
        name="fnet_stage3",
    )(z, consts["m3"], wo, bo, xs, mod)
    return out.reshape(B, S, D)


NR_CHUNK = 256


def _rope_tables(S):
    inv = 1.0 / (ROPE_THETA ** (jnp.arange(0, HEAD_DIM, 2, dtype=F32) / HEAD_DIM))
    ang = jnp.arange(S, dtype=F32)[:, None] * inv[None, :]
    cos, sin = jnp.cos(ang), jnp.sin(ang)
    cos128 = jnp.concatenate([cos, cos, cos, cos], axis=1)
    sin128 = jnp.concatenate([-sin, sin, -sin, sin], axis=1)
    return cos128, sin128


def _norm_rope(y, bd, gain, cos2, sin2, low):
    ms = _dot((y * y).astype(BF16), bd)
    yn = y * lax.rsqrt(ms + EPS) * gain
    up = pltpu.roll(yn, NR_CHUNK - HEAD_DIM // 2, axis=1)
    dn = pltpu.roll(yn, HEAD_DIM // 2, axis=1)
    return yn * cos2 + jnp.where(low, up, dn) * sin2


def _qkv_common(x_ref, mod_ref, ng_ref, cos_ref, sin_ref):
    shift, scale = mod_ref[0, 0:1, :], mod_ref[0, 1:2, :]
    h = _norm_mod(x_ref[0], ng_ref[...], scale, shift).astype(BF16)
    cos2 = jnp.concatenate([cos_ref[...]] * 2, axis=1)
    sin2 = jnp.concatenate([sin_ref[...]] * 2, axis=1)
    lane = lax.broadcasted_iota(jnp.int32, (1, NR_CHUNK), 1)
    low = (lane % HEAD_DIM) < HEAD_DIM // 2
    return h, cos2, sin2, low


def _swa_qkv_kernel(x_ref, mod_ref, ng_ref, w_ref, bd_ref, gq_ref, gk_ref, cos_ref, sin_ref,
                    q_ref, kv_ref):
    h, cos2, sin2, low = _qkv_common(x_ref, mod_ref, ng_ref, cos_ref, sin_ref)
    bd = bd_ref[...]
    nq = SWA_Q_HEADS * HEAD_DIM
    for c in range(nq // NR_CHUNK):
        y = _dot(h, w_ref[:, c * NR_CHUNK:(c + 1) * NR_CHUNK])
        q_ref[0, :, c * NR_CHUNK:(c + 1) * NR_CHUNK] = _norm_rope(
            y, bd, gq_ref[...], cos2, sin2, low).astype(BF16)
    k = _norm_rope(_dot(h, w_ref[:, nq:nq + NR_CHUNK]), bd, gk_ref[...], cos2, sin2, low)
    v = _dot(h, w_ref[:, nq + NR_CHUNK:nq + 2 * NR_CHUNK])
    lane = lax.broadcasted_iota(jnp.int32, (1, LANES), 1)
    lo = lane < HEAD_DIM
    k_cols, k_swaps, v_lo, v_hi = [], [], [], []
    for c in range(2):
        kc = k[:, c * LANES:(c + 1) * LANES]
        ks = pltpu.roll(kc, HEAD_DIM, axis=1)
        vc = v[:, c * LANES:(c + 1) * LANES]
        vs = pltpu.roll(vc, HEAD_DIM, axis=1)
        k_cols += [kc, ks]
        k_swaps += [ks, kc]
        v_lo += [jnp.where(lo, vc, 0.0), jnp.where(lo, vs, 0.0)]
        v_hi += [jnp.where(lo, 0.0, vs), jnp.where(lo, 0.0, vc)]
    kv_ref[0] = jnp.concatenate(k_cols + k_swaps + v_lo + v_hi, axis=1).astype(BF16)


def _diff_qkv_kernel(x_ref, mod_ref, ng_ref, wqt_ref, wk_ref, wvt_ref, bd_ref, gqt_ref, gk_ref,
                     cos_ref, sin_ref, cost_ref, sint_ref, qt_ref, k_ref, vt_ref):
    h, cos2, sin2, low = _qkv_common(x_ref, mod_ref, ng_ref, cos_ref, sin_ref)
    tm = h.shape[0]
    bd = bd_ref[...]
    n = DIFF_HEADS * 2 * HEAD_DIM
    heads = NR_CHUNK // HEAD_DIM
    half = HEAD_DIM // 2
    reps = tm // LANES
    gqt = jnp.concatenate([gqt_ref[...]] * reps, axis=1)
    cost, sint = cost_ref[...], sint_ref[...]
    for c in range(n // NR_CHUNK):
        yt = _dot_nt(wqt_ref[c * NR_CHUNK:(c + 1) * NR_CHUNK, :], h)
        y3 = yt.reshape(heads, HEAD_DIM, tm)
        ms = jnp.mean(y3 * y3, axis=1, keepdims=True)
        yn = y3 * lax.rsqrt(ms + EPS) * gqt
        partner = jnp.concatenate([yn[:, half:], yn[:, :half]], axis=1)
        qt = yn * cost + partner * sint
        qt_ref[0, 0, c * NR_CHUNK:(c + 1) * NR_CHUNK, :] = qt.reshape(NR_CHUNK, tm).astype(BF16)
    for c in range(n // NR_CHUNK):
        y = _dot(h, wk_ref[:, c * NR_CHUNK:(c + 1) * NR_CHUNK])
        k_ref[0, :, c * NR_CHUNK:(c + 1) * NR_CHUNK] = _norm_rope(
            y, bd, gk_ref[...], cos2, sin2, low).astype(BF16)
    vt_ref[0, 0] = _dot_nt(wvt_ref[...], h).astype(BF16)


def _diff_qkv(x, mod, ng, w, gq, gk, cos128, sin128, tm):
    B, S, D = x.shape
    n = DIFF_HEADS * 2 * HEAD_DIM
    nt = S // tm
    gq_s = gq.astype(F32) * (HEAD_DIM ** -0.5 * LOG2E)
    gqt = jnp.broadcast_to(gq_s.reshape(HEAD_DIM, 1), (HEAD_DIM, LANES))
    gk_t = jnp.tile(gk.astype(F32), NR_CHUNK // HEAD_DIM).reshape(1, NR_CHUNK)
    cost = cos128[:, :HEAD_DIM].T
    sint = sin128[:, :HEAD_DIM].T
    wqt = w[:, :n].T
    wk = w[:, n:2 * n]
    wvt = w[:, 2 * n:].T
    tshape = jax.ShapeDtypeStruct((B, nt, n, tm), BF16)
    return pl.pallas_call(
        _diff_qkv_kernel,
        grid=(B, nt),
        in_specs=[
            pl.BlockSpec((1, tm, D), lambda b, i: (b, i, 0)),
            pl.BlockSpec((1, 6, D), lambda b, i: (b, 0, 0)),
            _const_spec((1, D)),
            _const_spec((n, D)),
            _const_spec((D, n)),
            _const_spec((n, D)),
            _const_spec((NR_CHUNK, NR_CHUNK)),
            _const_spec((HEAD_DIM, LANES)),
            _const_spec((1, NR_CHUNK)),
            pl.BlockSpec((tm, LANES), lambda b, i: (i, 0)),
            pl.BlockSpec((tm, LANES), lambda b, i: (i, 0)),
            pl.BlockSpec((HEAD_DIM, tm), lambda b, i: (0, i)),
            pl.BlockSpec((HEAD_DIM, tm), lambda b, i: (0, i)),
        ],
        out_specs=[
            pl.BlockSpec((1, 1, n, tm), lambda b, i: (b, i, 0, 0)),
            pl.BlockSpec((1, tm, n), lambda b, i: (b, i, 0)),
            pl.BlockSpec((1, 1, n, tm), lambda b, i: (b, i, 0, 0)),
        ],
        out_shape=[tshape, jax.ShapeDtypeStruct((B, S, n), BF16), tshape],
        compiler_params=_cparams(("parallel", "parallel")),
        name="diff_qkv",
    )(x, mod, ng, wqt, wk, wvt, _seg_mean_matrix(), gqt, gk_t, cos128, sin128, cost, sint)


def _seg_mean_matrix():
    idx = np.arange(NR_CHUNK) // HEAD_DIM
    return jnp.asarray((idx[:, None] == idx[None, :]).astype(np.float32) / HEAD_DIM, BF16)


def _qkv(kernel, x, mod, ng, w, gq, gk, cos128, sin128, out_widths, tm=1024):
    B, S, D = x.shape
    scale = HEAD_DIM ** -0.5 * LOG2E
    gq_t = jnp.tile(gq.astype(F32) * scale, NR_CHUNK // HEAD_DIM).reshape(1, NR_CHUNK)
    gk_t = jnp.tile(gk.astype(F32), NR_CHUNK // HEAD_DIM).reshape(1, NR_CHUNK)
    return pl.pallas_call(
        kernel,
        grid=(B, S // tm),
        in_specs=[
            pl.BlockSpec((1, tm, D), lambda b, i: (b, i, 0)),
            pl.BlockSpec((1, 6, D), lambda b, i: (b, 0, 0)),
            _const_spec((1, D)),
            _const_spec(w.shape),
            _const_spec((NR_CHUNK, NR_CHUNK)),
            _const_spec((1, NR_CHUNK)),
            _const_spec((1, NR_CHUNK)),
            pl.BlockSpec((tm, LANES), lambda b, i: (i, 0)),
            pl.BlockSpec((tm, LANES), lambda b, i: (i, 0)),
        ],
        out_specs=[pl.BlockSpec((1, tm, n), lambda b, i: (b, i, 0)) for n in out_widths],
        out_shape=[jax.ShapeDtypeStruct((B, S, n), BF16) for n in out_widths],
        compiler_params=_cparams(("parallel", "parallel")),
        name=kernel.__name__.strip("_"),
    )(x, mod, ng, w, _seg_mean_matrix(), gq_t, gk_t, cos128, sin128)


def _swa_attn_kernel(sink_ref, q_ref, kvp_ref, kvm_ref, kvn_ref, x_ref, mod_ref, wo_ref, o_ref,
                     kv_scr, o_scr, *, tq, seq):
    i = pl.program_id(1)
    w = WINDOW
    kv_scr[0:w, :] = kvp_ref[0]
    kv_scr[w:w + tq, :] = kvm_ref[0]
    kv_scr[w + tq:w + tq + w, :] = kvn_ref[0]
    lane = lax.broadcasted_iota(jnp.int32, (1, LANES), 1)
    lo = lane < HEAD_DIM
    nk = 3 * w
    row = lax.broadcasted_iota(jnp.int32, (2 * w, nk), 0) % w
    col = lax.broadcasted_iota(jnp.int32, (2 * w, nk), 1) - w
    band = jnp.abs(row - col) <= WINDOW
    top = lax.broadcasted_iota(jnp.int32, (2 * w, 1), 0) < w
    kgrp = SWA_KV_HEADS * LANES

    def block(n, carry):
        r0 = pl.multiple_of(n * w, w)
        kpos = i * tq + n * w + col
        bias = jnp.where(band & (kpos >= 0) & (kpos < seq), 0.0, NEG)
        chains = [(hk, half) for hk in range(SWA_KV_HEADS) for half in range(2)]

        def scores(hk, half):
            c0 = 2 * hk * LANES
            qs = jnp.concatenate([q_ref[0, pl.ds(r0, w), c0:c0 + LANES],
                                  q_ref[0, pl.ds(r0, w), c0 + LANES:c0 + 2 * LANES]], axis=0)
            zero = jnp.zeros_like(qs)
            qm = jnp.where(lo, zero, qs) if half else jnp.where(lo, qs, zero)
            kc = half * kgrp + hk * LANES
            return _dot_nt(qm, kv_scr[pl.ds(r0, nk), kc:kc + LANES]) + bias

        ahead = scores(*chains[0])
        acc = None
        for idx, (hk, half) in enumerate(chains):
            s = ahead
            if idx + 1 < len(chains):
                ahead = scores(*chains[idx + 1])
            vc = (2 + half) * kgrp + hk * LANES
            vm = kv_scr[pl.ds(r0, nk), vc:vc + LANES]
            sk = jnp.where(top, sink_ref[4 * hk + half], sink_ref[4 * hk + 2 + half]) * LOG2E
            m = jnp.maximum(jnp.max(s, axis=-1, keepdims=True), sk)
            e = jnp.exp2(s - m)
            den = jnp.sum(e, axis=-1, keepdims=True) + jnp.exp2(sk - m)
            part = _dot(e.astype(BF16), vm) * (1.0 / den)
            if half == 0:
                acc = part
            else:
                acc = acc + part
                c0 = 2 * hk * LANES
                o_scr[pl.ds(r0, w), c0:c0 + LANES] = acc[:w].astype(BF16)
                o_scr[pl.ds(r0, w), c0 + LANES:c0 + 2 * LANES] = acc[w:].astype(BF16)
        return carry

    lax.fori_loop(0, tq // w, block, 0)
    y = _dot(o_scr[...], wo_ref[...])
    o_ref[0] = x_ref[0] + mod_ref[0, 2:3, :] * y


def _swa_attn(q, kv, x, mod, sink, wo, tq=512):
    B, S, D = x.shape
    w = WINDOW
    r = tq // w
    nkv = kv.shape[-1]
    return pl.pallas_call(
        functools.partial(_swa_attn_kernel, tq=tq, seq=S),
        grid=(B, S // tq),
        in_specs=[
            pl.BlockSpec(memory_space=pltpu.SMEM),
            pl.BlockSpec((1, tq, D), lambda b, i: (b, i, 0)),
            pl.BlockSpec((1, w, nkv), lambda b, i: (b, jnp.maximum(i * r - 1, 0), 0)),
            pl.BlockSpec((1, tq, nkv), lambda b, i: (b, i, 0)),
            pl.BlockSpec((1, w, nkv), lambda b, i: (b, jnp.minimum((i + 1) * r, S // w - 1), 0)),
            pl.BlockSpec((1, tq, D), lambda b, i: (b, i, 0)),
            pl.BlockSpec((1, 6, D), lambda b, i: (b, 0, 0)),
            _const_spec((D, D)),
        ],
        out_specs=pl.BlockSpec((1, tq, D), lambda b, i: (b, i, 0)),
        out_shape=jax.ShapeDtypeStruct((B, S, D), F32),
        scratch_shapes=[pltpu.VMEM((tq + 2 * w, nkv), BF16), pltpu.VMEM((tq, D), BF16)],
        compiler_params=_cparams(("parallel", "parallel")),
        name="swa_attention",
    )(sink, q, kv, kv, kv, x, mod, wo)


def _diff_attn_kernel(qt_ref, k_ref, vt_ref, lvec_ref, sgt_ref, o_ref, acc_scr, s_scr,
                      *, tq, tk, seq, lambda_init):
    rowi = lax.broadcasted_iota(jnp.int32, (LANES, 1), 0)
    first = rowi < HEAD_DIM
    qt = qt_ref[0, 0]
    zero = jnp.zeros_like(qt)
    qc = (jnp.where(first, qt, zero), jnp.where(first, zero, qt))
    acc_scr[...] = jnp.zeros(acc_scr.shape, F32)
    nsub = tq // DIFF_SUB
    chains = [(c, hq * DIFF_SUB) for c in range(2) for hq in range(nsub)]
    qsub = [qc[c][:, q0:q0 + DIFF_SUB] for c, q0 in chains]
    n = seq // tk

    def scores(j, slot, i):
        c, q0 = chains[i]
        k0 = pl.multiple_of(j * tk, tk)
        kb = k_ref[0, pl.ds(k0, tk), :]
        s = _dot(kb, qsub[i])
        s_scr[slot, c, :, q0:q0 + DIFF_SUB] = s
        return jnp.max(s, axis=0, keepdims=True)

    def probs_acc(j, slot, i, mcur, m, l):
        c, q0 = chains[i]
        vb = vt_ref[0, j]
        m_new = jnp.maximum(m, mcur)
        alpha = jnp.exp2(m - m_new)
        p = jnp.exp2(s_scr[slot, c, :, q0:q0 + DIFF_SUB] - m_new)
        l_new = alpha * l + jnp.sum(p, axis=0, keepdims=True)
        acc_scr[c, :, q0:q0 + DIFF_SUB] = (alpha * acc_scr[c, :, q0:q0 + DIFF_SUB]
                                           + _dot(vb, p.astype(BF16)))
        return m_new, l_new

    nc = len(chains)
    ms = tuple(jnp.full((1, DIFF_SUB), NEG, F32) for _ in chains)
    ls = tuple(jnp.zeros((1, DIFF_SUB), F32) for _ in chains)
    mcur = tuple(scores(0, 0, i) for i in range(nc))

    def body(j, slot, mcur, ms, ls):
        mc_new, ms_new, ls_new = [None] * nc, list(ms), list(ls)
        mc_new[0] = scores(j, slot, 0)
        for i in range(nc):
            if i + 1 < nc:
                mc_new[i + 1] = scores(j, slot, i + 1)
            ms_new[i], ls_new[i] = probs_acc(j - 1, 1 - slot, i, mcur[i], ms[i], ls[i])
        return tuple(mc_new), tuple(ms_new), tuple(ls_new)

    mcur, ms, ls = body(1, 1, mcur, ms, ls)

    def pair(t, carry):
        mcur, ms, ls = carry
        mcur, ms, ls = body(2 * t, 0, mcur, ms, ls)
        mcur, ms, ls = body(2 * t + 1, 1, mcur, ms, ls)
        return mcur, ms, ls

    mcur, ms, ls = lax.fori_loop(1, n // 2, pair, (mcur, ms, ls))
    ms, ls = list(ms), list(ls)
    for i in range(nc):
        ms[i], ls[i] = probs_acc(n - 1, 1, i, mcur[i], ms[i], ls[i])
    l0 = jnp.concatenate(ls[:nsub], axis=1)
    l1 = jnp.concatenate(ls[nsub:], axis=1)
    lv = lvec_ref[...]
    lam = (jnp.exp(jnp.sum(lv[0:1] * lv[1:2], axis=-1, keepdims=True))
           - jnp.exp(jnp.sum(lv[2:3] * lv[3:4], axis=-1, keepdims=True)) + lambda_init)
    ot = acc_scr[0] / l0 - lam * (acc_scr[1] / l1)
    ms = jnp.mean(ot * ot, axis=0, keepdims=True)
    sgt = jnp.concatenate([sgt_ref[...]] * (tq // LANES), axis=1)
    ot = ot * lax.rsqrt(ms + EPS) * sgt * (1.0 - lambda_init)
    o_ref[0] = ot.T.astype(BF16)


def _diff_attn(qt, k, vt, lvec, sgt, lambda_init, tq, tk):
    B, S, D = k.shape
    nchunk = S // tk
    return pl.pallas_call(
        functools.partial(_diff_attn_kernel, tq=tq, tk=tk, seq=S, lambda_init=lambda_init),
        grid=(B, DIFF_HEADS, S // tq),
        in_specs=[
            pl.BlockSpec((1, 1, LANES, tq), lambda b, h, i: (b, i, h, 0)),
            pl.BlockSpec((1, S, LANES), lambda b, h, i: (b, 0, h)),
            pl.BlockSpec((1, nchunk, LANES, tk), lambda b, h, i: (b, 0, h, 0)),
            _const_spec((4, LANES)),
            _const_spec((LANES, LANES)),
        ],
        out_specs=pl.BlockSpec((1, tq, LANES), lambda b, h, i: (b, i, h)),
        out_shape=jax.ShapeDtypeStruct((B, S, D), BF16),
        scratch_shapes=[pltpu.VMEM((2, LANES, tq), F32), pltpu.VMEM((2, 2, tk, tq), F32)],
        compiler_params=_cparams(("parallel", "parallel", "parallel")),
        name="diff_attention",
    )(qt, k, vt, lvec, sgt)


def _proj_res_kernel(o_ref, wo_ref, x_ref, mod_ref, out_ref):
    out_ref[0] = x_ref[0] + mod_ref[0, 2:3, :] * _dot(o_ref[0], wo_ref[...])


def _proj_res(o, wo, x, mod, tm=1024):
    B, S, D = x.shape
    return pl.pallas_call(
        _proj_res_kernel,
        grid=(B, S // tm),
        in_specs=[
            pl.BlockSpec((1, tm, D), lambda b, i: (b, i, 0)),
            _const_spec((D, D)),
            pl.BlockSpec((1, tm, D), lambda b, i: (b, i, 0)),
            pl.BlockSpec((1, 6, D), lambda b, i: (b, 0, 0)),
        ],
        out_specs=pl.BlockSpec((1, tm, D), lambda b, i: (b, i, 0)),
        out_shape=jax.ShapeDtypeStruct((B, S, D), F32),
        compiler_params=_cparams(("parallel", "parallel")),
        name="out_proj_residual",
    )(o, wo, x, mod)


def _trunk(x, mods, p):
    S = x.shape[1]
    cos128, sin128 = _rope_tables(S)
    fconsts = _fnet_consts(S)
    for i in range(DEPTH):
        mod = mods[i]
        ng1 = p["norm1_g"][i].reshape(1, D_MODEL)
        ng2 = p["norm2_g"][i].reshape(1, D_MODEL)
        kind, j = i % 3, i // 3
        if kind == 0:
            x = _fnet(x, mod, ng1, p["fnet_w"][j], p["fnet_b"][j].reshape(1, D_MODEL), fconsts)
        elif kind == 1:
            q, kv = _qkv(_swa_qkv_kernel, x, mod, ng1, p["swa_w_qkv"][j], p["swa_q_g"][j],
                         p["swa_k_g"][j], cos128, sin128,
                         (SWA_Q_HEADS * HEAD_DIM, 4 * SWA_KV_HEADS * LANES))
            x = _swa_attn(q, kv, x, mod, p["swa_sink"][j], p["swa_w_o"][j])
        else:
            lambda_init = 0.8 - 0.6 * math.exp(-0.3 * i)
            qt, k, vt = _diff_qkv(x, mod, ng1, p["diff_w_qkv"][j], p["diff_q_g"][j],
                                  p["diff_k_g"][j], cos128, sin128, DIFF_TILE)
            pad = ((0, 0), (0, LANES - HEAD_DIM))
            lvec = jnp.pad(jnp.stack([p["diff_lq1"][j], p["diff_lk1"][j],
                                      p["diff_lq2"][j], p["diff_lk2"][j]]).astype(F32), pad)
            sgt = jnp.broadcast_to(p["diff_subln_g"][j].astype(F32).reshape(LANES, 1),
                                   (LANES, LANES))
            o = _diff_attn(qt, k, vt, lvec, sgt, lambda_init, DIFF_TILE, DIFF_TILE)
            x = _proj_res(o, p["diff_w_o"][j], x, mod)
        x = _ffn(x, mod, ng2, p["ffn_w_gate"][i], p["ffn_w_up"][i], p["ffn_conv_w"][i],
                 p["ffn_conv_b"][i].reshape(1, D_FF), p["ffn_w_down"][i])
    return x


def kernel(x_prompt, x_sample, c_prompt, c_sample, ada_w, ada_b, norm1_g, norm2_g, fnet_w, fnet_b, swa_w_qkv, swa_q_g, swa_k_g, swa_sink, swa_w_o, diff_w_qkv, diff_q_g, diff_k_g, diff_lq1, diff_lk1, diff_lq2, diff_lk2, diff_subln_g, diff_w_o, ffn_w_gate, ffn_w_up, ffn_conv_w, ffn_conv_b, ffn_w_down):
    bp, bs = c_prompt.shape[0], c_sample.shape[0]
    rows = -(-(bp + bs) // SUBLANES) * SUBLANES
    c_all = jnp.concatenate(
        [c_prompt, c_sample, jnp.zeros((rows - bp - bs, D_MODEL), F32)], axis=0)
    mods = _ada_mod(c_all, ada_w, ada_b).reshape(DEPTH, rows, 6, D_MODEL)
    p = {
        "norm1_g": norm1_g, "norm2_g": norm2_g,
        "fnet_w": fnet_w.astype(BF16), "fnet_b": fnet_b,
        "swa_w_qkv": swa_w_qkv.astype(BF16), "swa_q_g": swa_q_g, "swa_k_g": swa_k_g,
        "swa_sink": swa_sink, "swa_w_o": swa_w_o.astype(BF16),
        "diff_w_qkv": diff_w_qkv.astype(BF16), "diff_q_g": diff_q_g, "diff_k_g": diff_k_g,
        "diff_lq1": diff_lq1, "diff_lk1": diff_lk1, "diff_lq2": diff_lq2, "diff_lk2": diff_lk2,
        "diff_subln_g": diff_subln_g, "diff_w_o": diff_w_o.astype(BF16),
        "ffn_w_gate": ffn_w_gate.astype(BF16), "ffn_w_up": ffn_w_up.astype(BF16),
        "ffn_conv_w": ffn_conv_w, "ffn_conv_b": ffn_conv_b,
        "ffn_w_down": ffn_w_down.astype(BF16),
    }
    y_prompt = _trunk(x_prompt, mods[:, :bp], p)
    y_sample = _trunk(x_sample, mods[:, bp:bp + bs], p)
    return (y_prompt, y_sample)
```

```python
import functools
import math

import numpy as np
import jax
import jax.numpy as jnp
from jax import lax
from jax.experimental import pallas as pl
from jax.experimental.pallas import tpu as pltpu

D_MODEL = 1024
DEPTH = 4
HEAD_DIM = 64
SWA_Q_HEADS = 16
SWA_KV_HEADS = 4
WINDOW = 128
DIFF_HEADS = 8
D_FF = 2816
FNET_GROUP_DIM = 256
ROPE_THETA = 10000.0
EPS = 1e-6
NEG = -1e30
LOG2E = math.log2(math.e)

LANES = 128
SUBLANES = 8
VMEM_LIMIT = 56 * 1024 * 1024
FNET_ROWS = 1024
DIFF_TILE = 1024
DIFF_SUB = 256
F32 = jnp.float32
BF16 = jnp.bfloat16


def _cparams(sem):
    return pltpu.CompilerParams(dimension_semantics=sem, vmem_limit_bytes=VMEM_LIMIT)


def _const_spec(shape):
    nd = len(shape)
    return pl.BlockSpec(shape, lambda *_: (0,) * nd, pipeline_mode=pl.Buffered(1))


def _norm_mod(x, gain, scale, shift):
    ms = jnp.mean(x * x, axis=-1, keepdims=True)
    return (x * lax.rsqrt(ms + EPS)) * gain * (1.0 + scale) + shift


def _dot(a, b):
    return jnp.dot(a, b, preferred_element_type=F32)


def _dot_nt(a, b):
    return lax.dot_general(a, b, (((1,), (1,)), ((), ())), preferred_element_type=F32)


def _ada_kernel(c_ref, w_ref, b_ref, o_ref):
    c = c_ref[...]
    ca = c / (1.0 + jnp.exp(-c))
    c_hi = ca.astype(BF16)
    c_lo = (ca - c_hi.astype(F32)).astype(BF16)
    w = w_ref[0]
    w_hi = w.astype(BF16)
    w_lo = (w - w_hi.astype(F32)).astype(BF16)
    o_ref[0] = _dot(c_hi, w_hi) + _dot(c_lo, w_hi) + _dot(c_hi, w_lo) + b_ref[0]


def _ada_mod(c_all, ada_w, ada_b):
    rows = c_all.shape[0]
    cn = 1536
    return pl.pallas_call(
        _ada_kernel,
        grid=(DEPTH, 6 * D_MODEL // cn),
        in_specs=[
            pl.BlockSpec((rows, D_MODEL), lambda l, j: (0, 0)),
            pl.BlockSpec((1, D_MODEL, cn), lambda l, j: (l, 0, j)),
            pl.BlockSpec((1, 1, cn), lambda l, j: (l, 0, j)),
        ],
        out_specs=pl.BlockSpec((1, rows, cn), lambda l, j: (l, 0, j)),
        out_shape=jax.ShapeDtypeStruct((DEPTH, rows, 6 * D_MODEL), F32),
        compiler_params=_cparams(("parallel", "parallel")),
        name="ada_mod",
    )(c_all, ada_w, ada_b.reshape(DEPTH, 1, 6 * D_MODEL))


FFN_CHUNKS = ((0, 1024), (1024, 2048), (2048, 2816))


def _ffn_kernel(x_ref, xp_ref, xn_ref, mod_ref, ng_ref, wg_ref, wu_ref, cw_ref, cb_ref,
                wd_ref, o_ref, *, tm):
    i = pl.program_id(1)
    last = pl.num_programs(1) - 1
    gain = ng_ref[...]
    shift, scale, gate = mod_ref[0, 3:4, :], mod_ref[0, 4:5, :], mod_ref[0, 5:6, :]
    x = x_ref[0]
    h = _norm_mod(x, gain, scale, shift).astype(BF16)
    xh = jnp.concatenate([xp_ref[0], xn_ref[0]], axis=0)
    hh = _norm_mod(xh, gain, scale, shift).astype(BF16)
    has_prev = (i > 0).astype(F32)
    has_next = (i < last).astype(F32)
    row = lax.broadcasted_iota(jnp.int32, (tm, 1), 0)
    acc = jnp.zeros((tm, D_MODEL), F32)

    def gate_up(c0, c1):
        return (_dot(h, wg_ref[:, c0:c1]), _dot(h, wu_ref[:, c0:c1]),
                _dot(hh, wg_ref[:, c0:c1]))

    ahead = gate_up(*FFN_CHUNKS[0])
    for k, (c0, c1) in enumerate(FFN_CHUNKS):
        g, u, gh = ahead
        if k + 1 < len(FFN_CHUNKS):
            ahead = gate_up(*FFN_CHUNKS[k + 1])
        g_before = gh[SUBLANES - 1:SUBLANES, :] * has_prev
        g_after = gh[SUBLANES:SUBLANES + 1, :] * has_next
        g_prev = jnp.where(row == 0, g_before, pltpu.roll(g, 1, axis=0))
        g_next = jnp.where(row == tm - 1, g_after, pltpu.roll(g, tm - 1, axis=0))
        gc = (g_prev * cw_ref[0:1, c0:c1] + g * cw_ref[1:2, c0:c1]
              + g_next * cw_ref[2:3, c0:c1] + cb_ref[:, c0:c1])
        act = (gc / (1.0 + jnp.exp(-gc))) * u
        acc = acc + _dot(act.astype(BF16), wd_ref[c0:c1, :])
    o_ref[0] = x + gate * acc


def _ffn(x, mod, ng, wg, wu, cw, cb, wd, tm=1024):
    B, S, D = x.shape
    nt = S // tm
    r = tm // SUBLANES
    return pl.pallas_call(
        functools.partial(_ffn_kernel, tm=tm),
        grid=(B, nt),
        in_specs=[
            pl.BlockSpec((1, tm, D), lambda b, i: (b, i, 0)),
            pl.BlockSpec((1, SUBLANES, D), lambda b, i: (b, jnp.maximum(i * r - 1, 0), 0)),
            pl.BlockSpec((1, SUBLANES, D),
                         lambda b, i: (b, jnp.minimum((i + 1) * r, S // SUBLANES - 1), 0)),
            pl.BlockSpec((1, 6, D), lambda b, i: (b, 0, 0)),
            _const_spec((1, D)),
            _const_spec((D, D_FF)),
            _const_spec((D, D_FF)),
            _const_spec((3, D_FF)),
            _const_spec((1, D_FF)),
            _const_spec((D_FF, D)),
        ],
        out_specs=pl.BlockSpec((1, tm, D), lambda b, i: (b, i, 0)),
        out_shape=jax.ShapeDtypeStruct((B, S, D), F32),
        compiler_params=_cparams(("parallel", "parallel")),
        name="conv_glu_ffn",
    )(x, x, x, mod, ng, wg, wu, cw, cb, wd)


FNET_N1 = 128


def _pack_pair(hi, lo):
    h = lax.bitcast_convert_type(hi.astype(BF16).astype(F32), jnp.uint32)
    l = lax.bitcast_convert_type(lo.astype(BF16).astype(F32), jnp.uint32)
    return h | (l >> 16)


def _unpack_pair(w):
    hi = lax.bitcast_convert_type(w & jnp.uint32(0xFFFF0000), F32)
    lo = lax.bitcast_convert_type(w << 16, F32)
    return hi.astype(BF16), lo.astype(BF16)


def _fnet_consts(S):
    n1 = FNET_N1
    n2 = S // n1
    jb = n2 // SUBLANES
    cc = np.arange(FNET_GROUP_DIM)
    ang = 2.0 * np.pi * ((cc[:, None] * cc[None, :]) % FNET_GROUP_DIM) / FNET_GROUP_DIM
    cs = np.concatenate([np.cos(ang), np.sin(ang)], axis=1) / math.sqrt(FNET_GROUP_DIM)
    t = np.arange(n1)
    ang1 = 2.0 * np.pi * ((t[:, None] * t[None, :]) % n1) / n1
    c1, s1 = np.cos(ang1), np.sin(ang1)
    m1 = np.block([[c1, -s1], [-s1, -c1]])
    t2 = np.arange(n2)
    angt = 2.0 * np.pi * ((t[:, None] * t2[None, :]) % S) / S
    twr = (np.cos(angt) / math.sqrt(S)).T.reshape(jb, SUBLANES, n1, 1)
    twi = (-np.sin(angt) / math.sqrt(S)).T.reshape(jb, SUBLANES, n1, 1)
    ang2 = 2.0 * np.pi * ((t2[:, None] * t2[None, :]) % n2) / n2
    direct = n2 >= LANES
    a = FNET_ROWS // n2
    if direct:
        m3 = np.concatenate([np.cos(ang2), np.sin(ang2)], axis=1)
    else:
        eye_a = np.eye(SUBLANES)

        def expand(m):
            m4 = m.reshape(n2, 1, 1, n2) * eye_a.reshape(1, SUBLANES, SUBLANES, 1)
            return m4.reshape(n2 * SUBLANES, SUBLANES * n2)

        m3 = np.concatenate([expand(np.cos(ang2)), expand(np.sin(ang2))], axis=1)
    return dict(
        n2=n2, a=a, jb=jb, direct=direct,
        cs=jnp.asarray(cs, BF16), m1=jnp.asarray(m1, BF16), m3=jnp.asarray(m3, BF16),
        twr=jnp.broadcast_to(jnp.asarray(twr, F32), (jb, SUBLANES, n1, LANES)),
        twi=jnp.broadcast_to(jnp.asarray(twi, F32), (jb, SUBLANES, n1, LANES)),
    )


def _fnet1_kernel(x_ref, mod_ref, ng_ref, cs_ref, m1_ref, twr_ref, twi_ref, z_ref, ab_scr,
                  zs_scr):
    n1 = FNET_N1
    rows = n1 * SUBLANES
    x = x_ref[0].reshape(rows, D_MODEL)
    shift, scale = mod_ref[0, 0:1, :], mod_ref[0, 1:2, :]
    h = _norm_mod(x, ng_ref[...], scale, shift).astype(BF16)
    g = FNET_GROUP_DIM
    reps = D_MODEL // LANES
    per = g // LANES
    for k in range(D_MODEL // g):
        p = _dot(h[:, k * g:(k + 1) * g], cs_ref[...])
        for c in range(per):
            ab_scr[0, k * per + c] = p[:, c * LANES:(c + 1) * LANES]
            ab_scr[1, k * per + c] = p[:, g + c * LANES:g + (c + 1) * LANES]
    for j in range(SUBLANES):
        sel = pl.ds(j, n1, stride=SUBLANES)
        ab = jnp.concatenate(
            [jnp.concatenate([ab_scr[half, c, sel, :] for c in range(reps)], axis=1)
             for half in range(2)], axis=0).astype(BF16)
        z = _dot(m1_ref[...], ab)
        zr, zi = z[:n1], z[n1:]
        twr = jnp.concatenate([twr_ref[0, j]] * reps, axis=1)
        twi = jnp.concatenate([twi_ref[0, j]] * reps, axis=1)
        packed = _pack_pair(zr * twr - zi * twi, zr * twi + zi * twr)
        for c in range(reps):
            zs_scr[c, sel, :] = packed[:, c * LANES:(c + 1) * LANES]
    z_ref[0] = jnp.concatenate([zs_scr[c] for c in range(reps)], axis=1).reshape(
        n1, SUBLANES, D_MODEL)


def _fnet3_kernel(z_ref, m3_ref, wo_ref, bo_ref, x_ref, mod_ref, o_ref, *scratch, a, n2,
                  direct):
    gate = mod_ref[0, 2:3, :]
    if direct:
        xs_scr, os_scr = scratch
        reps = D_MODEL // LANES
        x = x_ref[0].reshape(n2 * a, D_MODEL)
        for c in range(reps):
            xs_scr[c] = x[:, c * LANES:(c + 1) * LANES]
        ys = []
        for i in range(a):
            zr, zi = _unpack_pair(z_ref[0, i])
            ys.append(_dot(m3_ref[...], jnp.concatenate([zr, zi], axis=0)))
        y = jnp.concatenate(ys, axis=0)
        y = _dot(y.astype(BF16), wo_ref[...]) + bo_ref[...]
        for i in range(a):
            sel = pl.ds(i, n2, stride=a)
            xi = jnp.concatenate([xs_scr[c, sel, :] for c in range(reps)], axis=1)
            oi = xi + gate * y[i * n2:(i + 1) * n2]
            for c in range(reps):
                os_scr[c, sel, :] = oi[:, c * LANES:(c + 1) * LANES]
        o_ref[0] = jnp.concatenate([os_scr[c] for c in range(reps)], axis=1).reshape(
            n2, a, D_MODEL)
    else:
        g8 = SUBLANES
        rows = g8 * n2
        ys = []
        for i in range(a // g8):
            zr, zi = _unpack_pair(z_ref[0, i * g8:(i + 1) * g8].reshape(rows, D_MODEL))
            ys.append(_dot(m3_ref[...], jnp.concatenate([zr, zi], axis=0)))
        y = _dot(jnp.concatenate(ys, axis=0).astype(BF16), wo_ref[...]) + bo_ref[...]
        for i in range(a // g8):
            x = x_ref[0, :, i * g8:(i + 1) * g8, :].reshape(rows, D_MODEL)
            o_ref[0, :, i * g8:(i + 1) * g8, :] = (
                x + gate * y[i * rows:(i + 1) * rows]).reshape(n2, g8, D_MODEL)


def _fnet(x, mod, ng, wo, bo, consts):
    B, S, D = x.shape
    n1, n2, a, jb = FNET_N1, consts["n2"], consts["a"], consts["jb"]
    x4 = x.reshape(B, n1, n2, D)
    z = pl.pallas_call(
        _fnet1_kernel,
        grid=(B, jb),
        in_specs=[
            pl.BlockSpec((1, n1, SUBLANES, D), lambda b, j: (b, 0, j, 0)),
            pl.BlockSpec((1, 6, D), lambda b, j: (b, 0, 0)),
            _const_spec((1, D)),
            _const_spec((FNET_GROUP_DIM, 2 * FNET_GROUP_DIM)),
            _const_spec((2 * n1, 2 * n1)),
            pl.BlockSpec((1, SUBLANES, n1, LANES), lambda b, j: (j, 0, 0, 0)),
            pl.BlockSpec((1, SUBLANES, n1, LANES), lambda b, j: (j, 0, 0, 0)),
        ],
        out_specs=pl.BlockSpec((1, n1, SUBLANES, D), lambda b, j: (b, 0, j, 0)),
        out_shape=jax.ShapeDtypeStruct((B, n1, n2, D), jnp.uint32),
        scratch_shapes=[pltpu.VMEM((2, D // LANES, n1 * SUBLANES, LANES), F32),
                        pltpu.VMEM((D // LANES, n1 * SUBLANES, LANES), jnp.uint32)],
        compiler_params=_cparams(("parallel", "parallel")),
        name="fnet_stage1",
    )(x4, mod, ng, consts["cs"], consts["m1"], consts["twr"], consts["twi"])
    xs = x.reshape(B, n2, n1, D)
    out = pl.pallas_call(
        functools.partial(_fnet3_kernel, a=a, n2=n2, direct=consts["direct"]),
        grid=(B, n1 // a),
        in_specs=[
            pl.BlockSpec((1, a, n2, D), lambda b, i: (b, i, 0, 0)),
            _const_spec(consts["m3"].shape),
            _const_spec((D, D)),
            _const_spec((1, D)),
            pl.BlockSpec((1, n2, a, D), lambda b, i: (b, 0, i, 0)),
            pl.BlockSpec((1, 6, D), lambda b, i: (b, 0, 0)),
        ],
        out_specs=pl.BlockSpec((1, n2, a, D), lambda b, i: (b, 0, i, 0)),
        out_shape=jax.ShapeDtypeStruct((B, n2, n1, D), F32),
        scratch_shapes=([pltpu.VMEM((D // LANES, n2 * a, LANES), F32)] * 2
                        if consts["direct"] else []),
        compiler_params=_cparams(("parallel", "parallel")),
        name="fnet_stage3",
    )(z, consts["m3"], wo, bo, xs, mod)
    return out.reshape(B, S, D)


NR_CHUNK = 256


def _rope_tables(S):
    inv = 1.0 / (ROPE_THETA ** (jnp.arange(0, HEAD_DIM, 2, dtype=F32) / HEAD_DIM))
    ang = jnp.arange(S, dtype=F32)[:, None] * inv[None, :]
    cos, sin = jnp.cos(ang), jnp.sin(ang)
    cos128 = jnp.concatenate([cos, cos, cos, cos], axis=1)
    sin128 = jnp.concatenate([-sin, sin, -sin, sin], axis=1)
    return cos128, sin128


def _norm_rope(y, bd, gain, cos2, sin2, low):
    ms = _dot((y * y).astype(BF16), bd)
    yn = y * lax.rsqrt(ms + EPS) * gain
    up = pltpu.roll(yn, NR_CHUNK - HEAD_DIM // 2, axis=1)
    dn = pltpu.roll(yn, HEAD_DIM // 2, axis=1)
    return yn * cos2 + jnp.where(low, up, dn) * sin2


def _qkv_common(x_ref, mod_ref, ng_ref, cos_ref, sin_ref):
    shift, scale = mod_ref[0, 0:1, :], mod_ref[0, 1:2, :]
    h = _norm_mod(x_ref[0], ng_ref[...], scale, shift).astype(BF16)
    cos2 = jnp.concatenate([cos_ref[...]] * 2, axis=1)
    sin2 = jnp.concatenate([sin_ref[...]] * 2, axis=1)
    lane = lax.broadcasted_iota(jnp.int32, (1, NR_CHUNK), 1)
    low = (lane % HEAD_DIM) < HEAD_DIM // 2
    return h, cos2, sin2, low


def _swa_qkv_kernel(x_ref, mod_ref, ng_ref, w_ref, bd_ref, gq_ref, gk_ref, cos_ref, sin_ref,
                    q_ref, kv_ref):
    h, cos2, sin2, low = _qkv_common(x_ref, mod_ref, ng_ref, cos_ref, sin_ref)
    bd = bd_ref[...]
    nq = SWA_Q_HEADS * HEAD_DIM
    for c in range(nq // NR_CHUNK):
        y = _dot(h, w_ref[:, c * NR_CHUNK:(c + 1) * NR_CHUNK])
        q_ref[0, :, c * NR_CHUNK:(c + 1) * NR_CHUNK] = _norm_rope(
            y, bd, gq_ref[...], cos2, sin2, low).astype(BF16)
    k = _norm_rope(_dot(h, w_ref[:, nq:nq + NR_CHUNK]), bd, gk_ref[...], cos2, sin2, low)
    v = _dot(h, w_ref[:, nq + NR_CHUNK:nq + 2 * NR_CHUNK])
    lane = lax.broadcasted_iota(jnp.int32, (1, LANES), 1)
    lo = lane < HEAD_DIM
    k_cols, k_swaps, v_lo, v_hi = [], [], [], []
    for c in range(2):
        kc = k[:, c * LANES:(c + 1) * LANES]
        ks = pltpu.roll(kc, HEAD_DIM, axis=1)
        vc = v[:, c * LANES:(c + 1) * LANES]
        vs = pltpu.roll(vc, HEAD_DIM, axis=1)
        k_cols += [kc, ks]
        k_swaps += [ks, kc]
        v_lo += [jnp.where(lo, vc, 0.0), jnp.where(lo, vs, 0.0)]
        v_hi += [jnp.where(lo, 0.0, vs), jnp.where(lo, 0.0, vc)]
    kv_ref[0] = jnp.concatenate(k_cols + k_swaps + v_lo + v_hi, axis=1).astype(BF16)


def _diff_qkv_kernel(x_ref, mod_ref, ng_ref, wqt_ref, wk_ref, wvt_ref, bd_ref, gqt_ref, gk_ref,
                     cos_ref, sin_ref, cost_ref, sint_ref, qt_ref, k_ref, vt_ref):
    h, cos2, sin2, low = _qkv_common(x_ref, mod_ref, ng_ref, cos_ref, sin_ref)
    tm = h.shape[0]
    bd = bd_ref[...]
    n = DIFF_HEADS * 2 * HEAD_DIM
    heads = NR_CHUNK // HEAD_DIM
    half = HEAD_DIM // 2
    reps = tm // LANES
    gqt = jnp.concatenate([gqt_ref[...]] * reps, axis=1)
    cost, sint = cost_ref[...], sint_ref[...]
    for c in range(n // NR_CHUNK):
        yt = _dot_nt(wqt_ref[c * NR_CHUNK:(c + 1) * NR_CHUNK, :], h)
        y3 = yt.reshape(heads, HEAD_DIM, tm)
        ms = jnp.mean(y3 * y3, axis=1, keepdims=True)
        yn = y3 * lax.rsqrt(ms + EPS) * gqt
        partner = jnp.concatenate([yn[:, half:], yn[:, :half]], axis=1)
        qt = yn * cost + partner * sint
        qt_ref[0, 0, c * NR_CHUNK:(c + 1) * NR_CHUNK, :] = qt.reshape(NR_CHUNK, tm).astype(BF16)
    for c in range(n // NR_CHUNK):
        y = _dot(h, wk_ref[:, c * NR_CHUNK:(c + 1) * NR_CHUNK])
        k_ref[0, :, c * NR_CHUNK:(c + 1) * NR_CHUNK] = _norm_rope(
            y, bd, gk_ref[...], cos2, sin2, low).astype(BF16)
    vt_ref[0, 0] = _dot_nt(wvt_ref[...], h).astype(BF16)


def _diff_qkv(x, mod, ng, w, gq, gk, cos128, sin128, tm):
    B, S, D = x.shape
    n = DIFF_HEADS * 2 * HEAD_DIM
    nt = S // tm
    gq_s = gq.astype(F32) * (HEAD_DIM ** -0.5 * LOG2E)
    gqt = jnp.broadcast_to(gq_s.reshape(HEAD_DIM, 1), (HEAD_DIM, LANES))
    gk_t = jnp.tile(gk.astype(F32), NR_CHUNK // HEAD_DIM).reshape(1, NR_CHUNK)
    cost = cos128[:, :HEAD_DIM].T
    sint = sin128[:, :HEAD_DIM].T
    wqt = w[:, :n].T
    wk = w[:, n:2 * n]
    wvt = w[:, 2 * n:].T
    tshape = jax.ShapeDtypeStruct((B, nt, n, tm), BF16)
    return pl.pallas_call(
        _diff_qkv_kernel,
        grid=(B, nt),
        in_specs=[
            pl.BlockSpec((1, tm, D), lambda b, i: (b, i, 0)),
            pl.BlockSpec((1, 6, D), lambda b, i: (b, 0, 0)),
            _const_spec((1, D)),
            _const_spec((n, D)),
            _const_spec((D, n)),
            _const_spec((n, D)),
            _const_spec((NR_CHUNK, NR_CHUNK)),
            _const_spec((HEAD_DIM, LANES)),
            _const_spec((1, NR_CHUNK)),
            pl.BlockSpec((tm, LANES), lambda b, i: (i, 0)),
            pl.BlockSpec((tm, LANES), lambda b, i: (i, 0)),
            pl.BlockSpec((HEAD_DIM, tm), lambda b, i: (0, i)),
            pl.BlockSpec((HEAD_DIM, tm), lambda b, i: (0, i)),
        ],
        out_specs=[
            pl.BlockSpec((1, 1, n, tm), lambda b, i: (b, i, 0, 0)),
            pl.BlockSpec((1, tm, n), lambda b, i: (b, i, 0)),
            pl.BlockSpec((1, 1, n, tm), lambda b, i: (b, i, 0, 0)),
        ],
        out_shape=[tshape, jax.ShapeDtypeStruct((B, S, n), BF16), tshape],
        compiler_params=_cparams(("parallel", "parallel")),
        name="diff_qkv",
    )(x, mod, ng, wqt, wk, wvt, _seg_mean_matrix(), gqt, gk_t, cos128, sin128, cost, sint)


def _seg_mean_matrix():
    idx = np.arange(NR_CHUNK) // HEAD_DIM
    return jnp.asarray((idx[:, None] == idx[None, :]).astype(np.float32) / HEAD_DIM, BF16)


def _qkv(kernel, x, mod, ng, w, gq, gk, cos128, sin128, out_widths, tm=1024):
    B, S, D = x.shape
    scale = HEAD_DIM ** -0.5 * LOG2E
    gq_t = jnp.tile(gq.astype(F32) * scale, NR_CHUNK // HEAD_DIM).reshape(1, NR_CHUNK)
    gk_t = jnp.tile(gk.astype(F32), NR_CHUNK // HEAD_DIM).reshape(1, NR_CHUNK)
    return pl.pallas_call(
        kernel,
        grid=(B, S // tm),
        in_specs=[
            pl.BlockSpec((1, tm, D), lambda b, i: (b, i, 0)),
            pl.BlockSpec((1, 6, D), lambda b, i: (b, 0, 0)),
            _const_spec((1, D)),
            _const_spec(w.shape),
            _const_spec((NR_CHUNK, NR_CHUNK)),
            _const_spec((1, NR_CHUNK)),
            _const_spec((1, NR_CHUNK)),
            pl.BlockSpec((tm, LANES), lambda b, i: (i, 0)),
            pl.BlockSpec((tm, LANES), lambda b, i: (i, 0)),
        ],
        out_specs=[pl.BlockSpec((1, tm, n), lambda b, i: (b, i, 0)) for n in out_widths],
        out_shape=[jax.ShapeDtypeStruct((B, S, n), BF16) for n in out_widths],
        compiler_params=_cparams(("parallel", "parallel")),
        name=kernel.__name__.strip("_"),
    )(x, mod, ng, w, _seg_mean_matrix(), gq_t, gk_t, cos128, sin128)


def _swa_attn_kernel(sink_ref, q_ref, kvp_ref, kvm_ref, kvn_ref, x_ref, mod_ref, wo_ref, o_ref,
                     kv_scr, o_scr, *, tq, seq):
    i = pl.program_id(1)
    w = WINDOW
    kv_scr[0:w, :] = kvp_ref[0]
    kv_scr[w:w + tq, :] = kvm_ref[0]
    kv_scr[w + tq:w + tq + w, :] = kvn_ref[0]
    lane = lax.broadcasted_iota(jnp.int32, (1, LANES), 1)
    lo = lane < HEAD_DIM
    nk = 3 * w
    row = lax.broadcasted_iota(jnp.int32, (2 * w, nk), 0) % w
    col = lax.broadcasted_iota(jnp.int32, (2 * w, nk), 1) - w
    band = jnp.abs(row - col) <= WINDOW
    top = lax.broadcasted_iota(jnp.int32, (2 * w, 1), 0) < w
    kgrp = SWA_KV_HEADS * LANES

    def block(n, carry):
        r0 = pl.multiple_of(n * w, w)
        kpos = i * tq + n * w + col
        bias = jnp.where(band & (kpos >= 0) & (kpos < seq), 0.0, NEG)
        chains = [(hk, half) for hk in range(SWA_KV_HEADS) for half in range(2)]

        def scores(hk, half):
            c0 = 2 * hk * LANES
            qs = jnp.concatenate([q_ref[0, pl.ds(r0, w), c0:c0 + LANES],
                                  q_ref[0, pl.ds(r0, w), c0 + LANES:c0 + 2 * LANES]], axis=0)
            zero = jnp.zeros_like(qs)
            qm = jnp.where(lo, zero, qs) if half else jnp.where(lo, qs, zero)
            kc = half * kgrp + hk * LANES
            return _dot_nt(qm, kv_scr[pl.ds(r0, nk), kc:kc + LANES]) + bias

        ahead = scores(*chains[0])
        acc = None
        for idx, (hk, half) in enumerate(chains):
            s = ahead
            if idx + 1 < len(chains):
                ahead = scores(*chains[idx + 1])
            vc = (2 + half) * kgrp + hk * LANES
            vm = kv_scr[pl.ds(r0, nk), vc:vc + LANES]
            sk = jnp.where(top, sink_ref[4 * hk + half], sink_ref[4 * hk + 2 + half]) * LOG2E
            m = jnp.maximum(jnp.max(s, axis=-1, keepdims=True), sk)
            e = jnp.exp2(s - m)
            den = jnp.sum(e, axis=-1, keepdims=True) + jnp.exp2(sk - m)
            part = _dot(e.astype(BF16), vm) * (1.0 / den)
            if half == 0:
                acc = part
            else:
                acc = acc + part
                c0 = 2 * hk * LANES
                o_scr[pl.ds(r0, w), c0:c0 + LANES] = acc[:w].astype(BF16)
                o_scr[pl.ds(r0, w), c0 + LANES:c0 + 2 * LANES] = acc[w:].astype(BF16)
        return carry

    lax.fori_loop(0, tq // w, block, 0)
    y = _dot(o_scr[...], wo_ref[...])
    o_ref[0] = x_ref[0] + mod_ref[0, 2:3, :] * y


def _swa_attn(q, kv, x, mod, sink, wo, tq=512):
    B, S, D = x.shape
    w = WINDOW
    r = tq // w
    nkv = kv.shape[-1]
    return pl.pallas_call(
        functools.partial(_swa_attn_kernel, tq=tq, seq=S),
        grid=(B, S // tq),
        in_specs=[
            pl.BlockSpec(memory_space=pltpu.SMEM),
            pl.BlockSpec((1, tq, D), lambda b, i: (b, i, 0)),
            pl.BlockSpec((1, w, nkv), lambda b, i: (b, jnp.maximum(i * r - 1, 0), 0)),
            pl.BlockSpec((1, tq, nkv), lambda b, i: (b, i, 0)),
            pl.BlockSpec((1, w, nkv), lambda b, i: (b, jnp.minimum((i + 1) * r, S // w - 1), 0)),
            pl.BlockSpec((1, tq, D), lambda b, i: (b, i, 0)),
            pl.BlockSpec((1, 6, D), lambda b, i: (b, 0, 0)),
            _const_spec((D, D)),
        ],
        out_specs=pl.BlockSpec((1, tq, D), lambda b, i: (b, i, 0)),
        out_shape=jax.ShapeDtypeStruct((B, S, D), F32),
        scratch_shapes=[pltpu.VMEM((tq + 2 * w, nkv), BF16), pltpu.VMEM((tq, D), BF16)],
        compiler_params=_cparams(("parallel", "parallel")),
        name="swa_attention",
    )(sink, q, kv, kv, kv, x, mod, wo)


def _diff_attn_kernel(qt_ref, k_ref, vt_ref, lvec_ref, sgt_ref, o_ref, acc_scr, s_scr,
                      *, tq, tk, seq, lambda_init):
    rowi = lax.broadcasted_iota(jnp.int32, (LANES, 1), 0)
    first = rowi < HEAD_DIM
    qt = qt_ref[0, 0]
    zero = jnp.zeros_like(qt)
    qc = (jnp.where(first, qt, zero), jnp.where(first, zero, qt))
    acc_scr[...] = jnp.zeros(acc_scr.shape, F32)
    nsub = tq // DIFF_SUB
    chains = [(c, hq * DIFF_SUB) for c in range(2) for hq in range(nsub)]
    qsub = [qc[c][:, q0:q0 + DIFF_SUB] for c, q0 in chains]
    n = seq // tk

    def scores(j, slot, i):
        c, q0 = chains[i]
        k0 = pl.multiple_of(j * tk, tk)
        kb = k_ref[0, pl.ds(k0, tk), :]
        s = _dot(kb, qsub[i])
        s_scr[slot, c, :, q0:q0 + DIFF_SUB] = s
        return jnp.max(s, axis=0, keepdims=True)

    def probs_acc(j, slot, i, mcur, m, l):
        c, q0 = chains[i]
        half = (slot % (tq // tk)) * tk
        vb = vt_ref[0, j // (tq // tk), :, half:half + tk]
        m_new = jnp.maximum(m, mcur)
        alpha = jnp.exp2(m - m_new)
        p = jnp.exp2(s_scr[slot, c, :, q0:q0 + DIFF_SUB] - m_new)
        l_new = alpha * l + jnp.sum(p, axis=0, keepdims=True)
        acc_scr[c, :, q0:q0 + DIFF_SUB] = (alpha * acc_scr[c, :, q0:q0 + DIFF_SUB]
                                           + _dot(vb, p.astype(BF16)))
        return m_new, l_new

    nc = len(chains)
    ms = tuple(jnp.full((1, DIFF_SUB), NEG, F32) for _ in chains)
    ls = tuple(jnp.zeros((1, DIFF_SUB), F32) for _ in chains)
    mcur = tuple(scores(0, 0, i) for i in range(nc))

    def body(j, slot, mcur, ms, ls):
        mc_new, ms_new, ls_new = [None] * nc, list(ms), list(ls)
        mc_new[0] = scores(j, slot, 0)
        for i in range(nc):
            if i + 1 < nc:
                mc_new[i + 1] = scores(j, slot, i + 1)
            ms_new[i], ls_new[i] = probs_acc(j - 1, 1 - slot, i, mcur[i], ms[i], ls[i])
        return tuple(mc_new), tuple(ms_new), tuple(ls_new)

    mcur, ms, ls = body(1, 1, mcur, ms, ls)

    def pair(t, carry):
        mcur, ms, ls = carry
        mcur, ms, ls = body(2 * t, 0, mcur, ms, ls)
        mcur, ms, ls = body(2 * t + 1, 1, mcur, ms, ls)
        return mcur, ms, ls

    mcur, ms, ls = lax.fori_loop(1, n // 2, pair, (mcur, ms, ls))
    ms, ls = list(ms), list(ls)
    for i in range(nc):
        ms[i], ls[i] = probs_acc(n - 1, 1, i, mcur[i], ms[i], ls[i])
    l0 = jnp.concatenate(ls[:nsub], axis=1)
    l1 = jnp.concatenate(ls[nsub:], axis=1)
    lv = lvec_ref[...]
    lam = (jnp.exp(jnp.sum(lv[0:1] * lv[1:2], axis=-1, keepdims=True))
           - jnp.exp(jnp.sum(lv[2:3] * lv[3:4], axis=-1, keepdims=True)) + lambda_init)
    ot = acc_scr[0] / l0 - lam * (acc_scr[1] / l1)
    ms = jnp.mean(ot * ot, axis=0, keepdims=True)
    sgt = jnp.concatenate([sgt_ref[...]] * (tq // LANES), axis=1)
    ot = ot * lax.rsqrt(ms + EPS) * sgt * (1.0 - lambda_init)
    o_ref[0] = ot.T.astype(BF16)


def _diff_attn(qt, k, vt, lvec, sgt, lambda_init, tq, tk):
    B, S, D = k.shape
    assert tq // tk in (1, 2) and tq % tk == 0
    nchunk = S // tq
    return pl.pallas_call(
        functools.partial(_diff_attn_kernel, tq=tq, tk=tk, seq=S, lambda_init=lambda_init),
        grid=(B, DIFF_HEADS, S // tq),
        in_specs=[
            pl.BlockSpec((1, 1, LANES, tq), lambda b, h, i: (b, i, h, 0)),
            pl.BlockSpec((1, S, LANES), lambda b, h, i: (b, 0, h)),
            pl.BlockSpec((1, nchunk, LANES, tq), lambda b, h, i: (b, 0, h, 0)),
            _const_spec((4, LANES)),
            _const_spec((LANES, LANES)),
        ],
        out_specs=pl.BlockSpec((1, tq, LANES), lambda b, h, i: (b, i, h)),
        out_shape=jax.ShapeDtypeStruct((B, S, D), BF16),
        scratch_shapes=[pltpu.VMEM((2, LANES, tq), F32), pltpu.VMEM((2, 2, tk, tq), F32)],
        compiler_params=_cparams(("parallel", "parallel", "parallel")),
        name="diff_attention",
    )(qt, k, vt, lvec, sgt)


def _proj_res_kernel(o_ref, wo_ref, x_ref, mod_ref, out_ref):
    out_ref[0] = x_ref[0] + mod_ref[0, 2:3, :] * _dot(o_ref[0], wo_ref[...])


def _proj_res(o, wo, x, mod, tm=1024):
    B, S, D = x.shape
    return pl.pallas_call(
        _proj_res_kernel,
        grid=(B, S // tm),
        in_specs=[
            pl.BlockSpec((1, tm, D), lambda b, i: (b, i, 0)),
            _const_spec((D, D)),
            pl.BlockSpec((1, tm, D), lambda b, i: (b, i, 0)),
            pl.BlockSpec((1, 6, D), lambda b, i: (b, 0, 0)),
        ],
        out_specs=pl.BlockSpec((1, tm, D), lambda b, i: (b, i, 0)),
        out_shape=jax.ShapeDtypeStruct((B, S, D), F32),
        compiler_params=_cparams(("parallel", "parallel")),
        name="out_proj_residual",
    )(o, wo, x, mod)


def _trunk(x, mods, p):
    S = x.shape[1]
    cos128, sin128 = _rope_tables(S)
    fconsts = _fnet_consts(S)
    for i in range(DEPTH):
        mod = mods[i]
        ng1 = p["norm1_g"][i].reshape(1, D_MODEL)
        ng2 = p["norm2_g"][i].reshape(1, D_MODEL)
        kind, j = i % 3, i // 3
        if kind == 0:
            x = _fnet(x, mod, ng1, p["fnet_w"][j], p["fnet_b"][j].reshape(1, D_MODEL), fconsts)
        elif kind == 1:
            q, kv = _qkv(_swa_qkv_kernel, x, mod, ng1, p["swa_w_qkv"][j], p["swa_q_g"][j],
                         p["swa_k_g"][j], cos128, sin128,
                         (SWA_Q_HEADS * HEAD_DIM, 4 * SWA_KV_HEADS * LANES))
            x = _swa_attn(q, kv, x, mod, p["swa_sink"][j], p["swa_w_o"][j])
        else:
            lambda_init = 0.8 - 0.6 * math.exp(-0.3 * i)
            qt, k, vt = _diff_qkv(x, mod, ng1, p["diff_w_qkv"][j], p["diff_q_g"][j],
                                  p["diff_k_g"][j], cos128, sin128, DIFF_TILE)
            pad = ((0, 0), (0, LANES - HEAD_DIM))
            lvec = jnp.pad(jnp.stack([p["diff_lq1"][j], p["diff_lk1"][j],
                                      p["diff_lq2"][j], p["diff_lk2"][j]]).astype(F32), pad)
            sgt = jnp.broadcast_to(p["diff_subln_g"][j].astype(F32).reshape(LANES, 1),
                                   (LANES, LANES))
            tk = DIFF_TILE if S // DIFF_TILE >= 4 else DIFF_TILE // 2
            o = _diff_attn(qt, k, vt, lvec, sgt, lambda_init, DIFF_TILE, tk)
            x = _proj_res(o, p["diff_w_o"][j], x, mod)
        x = _ffn(x, mod, ng2, p["ffn_w_gate"][i], p["ffn_w_up"][i], p["ffn_conv_w"][i],
                 p["ffn_conv_b"][i].reshape(1, D_FF), p["ffn_w_down"][i])
    return x


def kernel(x_prompt, x_sample, c_prompt, c_sample, ada_w, ada_b, norm1_g, norm2_g, fnet_w, fnet_b, swa_w_qkv, swa_q_g, swa_k_g, swa_sink, swa_w_o, diff_w_qkv, diff_q_g, diff_k_g, diff_lq1, diff_lk1, diff_lq2, diff_lk2, diff_subln_g, diff_w_o, ffn_w_gate, ffn_w_up, ffn_conv_w, ffn_conv_b, ffn_w_down):
    bp, bs = c_prompt.shape[0], c_sample.shape[0]
    rows = -(-(bp + bs) // SUBLANES) * SUBLANES
    c_all = jnp.concatenate(
        [c_prompt, c_sample, jnp.zeros((rows - bp - bs, D_MODEL), F32)], axis=0)
    mods = _ada_mod(c_all, ada_w, ada_b).reshape(DEPTH, rows, 6, D_MODEL)
    p = {
        "norm1_g": norm1_g, "norm2_g": norm2_g,
        "fnet_w": fnet_w.astype(BF16), "fnet_b": fnet_b,
        "swa_w_qkv": swa_w_qkv.astype(BF16), "swa_q_g": swa_q_g, "swa_k_g": swa_k_g,
        "swa_sink": swa_sink, "swa_w_o": swa_w_o.astype(BF16),
        "diff_w_qkv": diff_w_qkv.astype(BF16), "diff_q_g": diff_q_g, "diff_k_g": diff_k_g,
        "diff_lq1": diff_lq1, "diff_lk1": diff_lk1, "diff_lq2": diff_lq2, "diff_lk2": diff_lk2,
        "diff_subln_g": diff_subln_g, "diff_w_o": diff_w_o.astype(BF16),
        "ffn_w_gate": ffn_w_gate.astype(BF16), "ffn_w_up": ffn_w_up.astype(BF16),
        "ffn_conv_w": ffn_conv_w, "ffn_conv_b": ffn_conv_b,
        "ffn_w_down": ffn_w_down.astype(BF16),
    }
    y_prompt = _trunk(x_prompt, mods[:, :bp], p)
    y_sample = _trunk(x_sample, mods[:, bp:bp + bs], p)
    return (y_prompt, y_sample)
```

```python
import functools
import math

import numpy as np
import jax
import jax.numpy as jnp
from jax import lax
from jax.experimental import pallas as pl
from jax.experimental.pallas import tpu as pltpu

D_MODEL = 1024
DEPTH = 4
HEAD_DIM = 64
SWA_Q_HEADS = 16
SWA_KV_HEADS = 4
WINDOW = 128
DIFF_HEADS = 8
D_FF = 2816
FNET_GROUP_DIM = 256
ROPE_THETA = 10000.0
EPS = 1e-6
NEG = -1e30
LOG2E = math.log2(math.e)

LANES = 128
SUBLANES = 8
VMEM_LIMIT = 56 * 1024 * 1024
FNET_ROWS = 1024
DIFF_TILE = 1024
DIFF_SUB = 256
F32 = jnp.float32
BF16 = jnp.bfloat16


def _cparams(sem):
    return pltpu.CompilerParams(dimension_semantics=sem, vmem_limit_bytes=VMEM_LIMIT)


def _const_spec(shape):
    nd = len(shape)
    return pl.BlockSpec(shape, lambda *_: (0,) * nd, pipeline_mode=pl.Buffered(1))


def _norm_mod(x, gain, scale, shift):
    ms = jnp.mean(x * x, axis=-1, keepdims=True)
    return (x * lax.rsqrt(ms + EPS)) * gain * (1.0 + scale) + shift


def _dot(a, b):
    return jnp.dot(a, b, preferred_element_type=F32)


def _dot_nt(a, b):
    return lax.dot_general(a, b, (((1,), (1,)), ((), ())), preferred_element_type=F32)


def _ada_kernel(c_ref, w_ref, b_ref, o_ref):
    c = c_ref[...]
    ca = c / (1.0 + jnp.exp(-c))
    c_hi = ca.astype(BF16)
    c_lo = (ca - c_hi.astype(F32)).astype(BF16)
    w = w_ref[0]
    w_hi = w.astype(BF16)
    w_lo = (w - w_hi.astype(F32)).astype(BF16)
    o_ref[0] = _dot(c_hi, w_hi) + _dot(c_lo, w_hi) + _dot(c_hi, w_lo) + b_ref[0]


def _ada_mod(c_all, ada_w, ada_b):
    rows = c_all.shape[0]
    cn = 1536
    return pl.pallas_call(
        _ada_kernel,
        grid=(DEPTH, 6 * D_MODEL // cn),
        in_specs=[
            pl.BlockSpec((rows, D_MODEL), lambda l, j: (0, 0)),
            pl.BlockSpec((1, D_MODEL, cn), lambda l, j: (l, 0, j)),
            pl.BlockSpec((1, 1, cn), lambda l, j: (l, 0, j)),
        ],
        out_specs=pl.BlockSpec((1, rows, cn), lambda l, j: (l, 0, j)),
        out_shape=jax.ShapeDtypeStruct((DEPTH, rows, 6 * D_MODEL), F32),
        compiler_params=_cparams(("parallel", "parallel")),
        name="ada_mod",
    )(c_all, ada_w, ada_b.reshape(DEPTH, 1, 6 * D_MODEL))


FFN_CHUNKS = ((0, 1024), (1024, 2048), (2048, 2816))


def _ffn_kernel(x_ref, xp_ref, xn_ref, mod_ref, ng_ref, wg_ref, wu_ref, cw_ref, cb_ref,
                wd_ref, o_ref, *, tm):
    i = pl.program_id(1)
    last = pl.num_programs(1) - 1
    gain = ng_ref[...]
    shift, scale, gate = mod_ref[0, 3:4, :], mod_ref[0, 4:5, :], mod_ref[0, 5:6, :]
    x = x_ref[0]
    hm = tm // 2
    ha = _norm_mod(x[:hm], gain, scale, shift).astype(BF16)
    hb = _norm_mod(x[hm:], gain, scale, shift).astype(BF16)
    xh = jnp.concatenate([xp_ref[0], xn_ref[0]], axis=0)
    hh = _norm_mod(xh, gain, scale, shift).astype(BF16)
    has_prev = (i > 0).astype(F32)
    has_next = (i < last).astype(F32)
    row = lax.broadcasted_iota(jnp.int32, (tm, 1), 0)
    acc = jnp.zeros((tm, D_MODEL), F32)

    def gate_up(c0, c1):
        g = jnp.concatenate([_dot(ha, wg_ref[:, c0:c1]), _dot(hb, wg_ref[:, c0:c1])], axis=0)
        u = jnp.concatenate([_dot(ha, wu_ref[:, c0:c1]), _dot(hb, wu_ref[:, c0:c1])], axis=0)
        return g, u, _dot(hh, wg_ref[:, c0:c1])

    ahead = gate_up(*FFN_CHUNKS[0])
    for k, (c0, c1) in enumerate(FFN_CHUNKS):
        g, u, gh = ahead
        if k + 1 < len(FFN_CHUNKS):
            ahead = gate_up(*FFN_CHUNKS[k + 1])
        g_before = gh[SUBLANES - 1:SUBLANES, :] * has_prev
        g_after = gh[SUBLANES:SUBLANES + 1, :] * has_next
        g_prev = jnp.where(row == 0, g_before, pltpu.roll(g, 1, axis=0))
        g_next = jnp.where(row == tm - 1, g_after, pltpu.roll(g, tm - 1, axis=0))
        gc = (g_prev * cw_ref[0:1, c0:c1] + g * cw_ref[1:2, c0:c1]
              + g_next * cw_ref[2:3, c0:c1] + cb_ref[:, c0:c1])
        act = (gc / (1.0 + jnp.exp(-gc))) * u
        acc = acc + _dot(act.astype(BF16), wd_ref[c0:c1, :])
    o_ref[0] = x + gate * acc


def _ffn(x, mod, ng, wg, wu, cw, cb, wd, tm=1024):
    B, S, D = x.shape
    nt = S // tm
    r = tm // SUBLANES
    return pl.pallas_call(
        functools.partial(_ffn_kernel, tm=tm),
        grid=(B, nt),
        in_specs=[
            pl.BlockSpec((1, tm, D), lambda b, i: (b, i, 0)),
            pl.BlockSpec((1, SUBLANES, D), lambda b, i: (b, jnp.maximum(i * r - 1, 0), 0)),
            pl.BlockSpec((1, SUBLANES, D),
                         lambda b, i: (b, jnp.minimum((i + 1) * r, S // SUBLANES - 1), 0)),
            pl.BlockSpec((1, 6, D), lambda b, i: (b, 0, 0)),
            _const_spec((1, D)),
            _const_spec((D, D_FF)),
            _const_spec((D, D_FF)),
            _const_spec((3, D_FF)),
            _const_spec((1, D_FF)),
            _const_spec((D_FF, D)),
        ],
        out_specs=pl.BlockSpec((1, tm, D), lambda b, i: (b, i, 0)),
        out_shape=jax.ShapeDtypeStruct((B, S, D), F32),
        compiler_params=_cparams(("parallel", "parallel")),
        name="conv_glu_ffn",
    )(x, x, x, mod, ng, wg, wu, cw, cb, wd)


FNET_N1 = 128


def _pack_pair(hi, lo):
    h = lax.bitcast_convert_type(hi.astype(BF16).astype(F32), jnp.uint32)
    l = lax.bitcast_convert_type(lo.astype(BF16).astype(F32), jnp.uint32)
    return h | (l >> 16)


def _unpack_pair(w):
    hi = lax.bitcast_convert_type(w & jnp.uint32(0xFFFF0000), F32)
    lo = lax.bitcast_convert_type(w << 16, F32)
    return hi.astype(BF16), lo.astype(BF16)


def _fnet_consts(S):
    n1 = FNET_N1
    n2 = S // n1
    jb = n2 // SUBLANES
    cc = np.arange(FNET_GROUP_DIM)
    ang = 2.0 * np.pi * ((cc[:, None] * cc[None, :]) % FNET_GROUP_DIM) / FNET_GROUP_DIM
    cs = np.concatenate([np.cos(ang), np.sin(ang)], axis=1) / math.sqrt(FNET_GROUP_DIM)
    t = np.arange(n1)
    ang1 = 2.0 * np.pi * ((t[:, None] * t[None, :]) % n1) / n1
    c1, s1 = np.cos(ang1), np.sin(ang1)
    m1 = np.block([[c1, -s1], [-s1, -c1]])
    t2 = np.arange(n2)
    angt = 2.0 * np.pi * ((t[:, None] * t2[None, :]) % S) / S
    twr = (np.cos(angt) / math.sqrt(S)).T.reshape(jb, SUBLANES, n1, 1)
    twi = (-np.sin(angt) / math.sqrt(S)).T.reshape(jb, SUBLANES, n1, 1)
    ang2 = 2.0 * np.pi * ((t2[:, None] * t2[None, :]) % n2) / n2
    direct = n2 >= LANES
    a = FNET_ROWS // n2
    if direct:
        m3 = np.concatenate([np.cos(ang2), np.sin(ang2)], axis=1)
    else:
        eye_a = np.eye(SUBLANES)

        def expand(m):
            m4 = m.reshape(n2, 1, 1, n2) * eye_a.reshape(1, SUBLANES, SUBLANES, 1)
            return m4.reshape(n2 * SUBLANES, SUBLANES * n2)

        m3 = np.concatenate([expand(np.cos(ang2)), expand(np.sin(ang2))], axis=1)
    return dict(
        n2=n2, a=a, jb=jb, direct=direct,
        cs=jnp.asarray(cs, BF16), m1=jnp.asarray(m1, BF16), m3=jnp.asarray(m3, BF16),
        twr=jnp.broadcast_to(jnp.asarray(twr, F32), (jb, SUBLANES, n1, LANES)),
        twi=jnp.broadcast_to(jnp.asarray(twi, F32), (jb, SUBLANES, n1, LANES)),
    )


def _fnet1_kernel(x_ref, mod_ref, ng_ref, cs_ref, m1_ref, twr_ref, twi_ref, z_ref, ab_scr,
                  zs_scr):
    n1 = FNET_N1
    rows = n1 * SUBLANES
    x = x_ref[0].reshape(rows, D_MODEL)
    shift, scale = mod_ref[0, 0:1, :], mod_ref[0, 1:2, :]
    h = _norm_mod(x, ng_ref[...], scale, shift).astype(BF16)
    g = FNET_GROUP_DIM
    reps = D_MODEL // LANES
    per = g // LANES
    for k in range(D_MODEL // g):
        p = _dot(h[:, k * g:(k + 1) * g], cs_ref[...])
        for c in range(per):
            ab_scr[0, k * per + c] = p[:, c * LANES:(c + 1) * LANES]
            ab_scr[1, k * per + c] = p[:, g + c * LANES:g + (c + 1) * LANES]
    for j in range(SUBLANES):
        sel = pl.ds(j, n1, stride=SUBLANES)
        ab = jnp.concatenate(
            [jnp.concatenate([ab_scr[half, c, sel, :] for c in range(reps)], axis=1)
             for half in range(2)], axis=0).astype(BF16)
        z = _dot(m1_ref[...], ab)
        zr, zi = z[:n1], z[n1:]
        twr = jnp.concatenate([twr_ref[0, j]] * reps, axis=1)
        twi = jnp.concatenate([twi_ref[0, j]] * reps, axis=1)
        packed = _pack_pair(zr * twr - zi * twi, zr * twi + zi * twr)
        for c in range(reps):
            zs_scr[c, sel, :] = packed[:, c * LANES:(c + 1) * LANES]
    z_ref[0] = jnp.concatenate([zs_scr[c] for c in range(reps)], axis=1).reshape(
        n1, SUBLANES, D_MODEL)


def _fnet3_kernel(z_ref, m3_ref, wo_ref, bo_ref, x_ref, mod_ref, o_ref, *scratch, a, n2,
                  direct):
    gate = mod_ref[0, 2:3, :]
    if direct:
        xs_scr, os_scr = scratch
        reps = D_MODEL // LANES
        x = x_ref[0].reshape(n2 * a, D_MODEL)
        for c in range(reps):
            xs_scr[c] = x[:, c * LANES:(c + 1) * LANES]
        ys = []
        for i in range(a):
            zr, zi = _unpack_pair(z_ref[0, i])
            ys.append(_dot(m3_ref[...], jnp.concatenate([zr, zi], axis=0)))
        y = jnp.concatenate(ys, axis=0)
        y = _dot(y.astype(BF16), wo_ref[...]) + bo_ref[...]
        for i in range(a):
            sel = pl.ds(i, n2, stride=a)
            xi = jnp.concatenate([xs_scr[c, sel, :] for c in range(reps)], axis=1)
            oi = xi + gate * y[i * n2:(i + 1) * n2]
            for c in range(reps):
                os_scr[c, sel, :] = oi[:, c * LANES:(c + 1) * LANES]
        o_ref[0] = jnp.concatenate([os_scr[c] for c in range(reps)], axis=1).reshape(
            n2, a, D_MODEL)
    else:
        g8 = SUBLANES
        rows = g8 * n2
        ys = []
        for i in range(a // g8):
            zr, zi = _unpack_pair(z_ref[0, i * g8:(i + 1) * g8].reshape(rows, D_MODEL))
            ys.append(_dot(m3_ref[...], jnp.concatenate([zr, zi], axis=0)))
        y = _dot(jnp.concatenate(ys, axis=0).astype(BF16), wo_ref[...]) + bo_ref[...]
        for i in range(a // g8):
            x = x_ref[0, :, i * g8:(i + 1) * g8, :].reshape(rows, D_MODEL)
            o_ref[0, :, i * g8:(i + 1) * g8, :] = (
                x + gate * y[i * rows:(i + 1) * rows]).reshape(n2, g8, D_MODEL)


def _fnet(x, mod, ng, wo, bo, consts):
    B, S, D = x.shape
    n1, n2, a, jb = FNET_N1, consts["n2"], consts["a"], consts["jb"]
    x4 = x.reshape(B, n1, n2, D)
    z = pl.pallas_call(
        _fnet1_kernel,
        grid=(B, jb),
        in_specs=[
            pl.BlockSpec((1, n1, SUBLANES, D), lambda b, j: (b, 0, j, 0)),
            pl.BlockSpec((1, 6, D), lambda b, j: (b, 0, 0)),
            _const_spec((1, D)),
            _const_spec((FNET_GROUP_DIM, 2 * FNET_GROUP_DIM)),
            _const_spec((2 * n1, 2 * n1)),
            pl.BlockSpec((1, SUBLANES, n1, LANES), lambda b, j: (j, 0, 0, 0)),
            pl.BlockSpec((1, SUBLANES, n1, LANES), lambda b, j: (j, 0, 0, 0)),
        ],
        out_specs=pl.BlockSpec((1, n1, SUBLANES, D), lambda b, j: (b, 0, j, 0)),
        out_shape=jax.ShapeDtypeStruct((B, n1, n2, D), jnp.uint32),
        scratch_shapes=[pltpu.VMEM((2, D // LANES, n1 * SUBLANES, LANES), F32),
                        pltpu.VMEM((D // LANES, n1 * SUBLANES, LANES), jnp.uint32)],
        compiler_params=_cparams(("parallel", "parallel")),
        name="fnet_stage1",
    )(x4, mod, ng, consts["cs"], consts["m1"], consts["twr"], consts["twi"])
    xs = x.reshape(B, n2, n1, D)
    out = pl.pallas_call(
        functools.partial(_fnet3_kernel, a=a, n2=n2, direct=consts["direct"]),
        grid=(B, n1 // a),
        in_specs=[
            pl.BlockSpec((1, a, n2, D), lambda b, i: (b, i, 0, 0)),
            _const_spec(consts["m3"].shape),
            _const_spec((D, D)),
            _const_spec((1, D)),
            pl.BlockSpec((1, n2, a, D), lambda b, i: (b, 0, i, 0)),
            pl.BlockSpec((1, 6, D), lambda b, i: (b, 0, 0)),
        ],
        out_specs=pl.BlockSpec((1, n2, a, D), lambda b, i: (b, 0, i, 0)),
        out_shape=jax.ShapeDtypeStruct((B, n2, n1, D), F32),
        scratch_shapes=([pltpu.VMEM((D // LANES, n2 * a, LANES), F32)] * 2
                        if consts["direct"] else []),
        compiler_params=_cparams(("parallel", "parallel")),
        name="fnet_stage3",
    )(z, consts["m3"], wo, bo, xs, mod)
    return out.reshape(B, S, D)


NR_CHUNK = 256


def _rope_tables(S):
    inv = 1.0 / (ROPE_THETA ** (jnp.arange(0, HEAD_DIM, 2, dtype=F32) / HEAD_DIM))
    ang = jnp.arange(S, dtype=F32)[:, None] * inv[None, :]
    cos, sin = jnp.cos(ang), jnp.sin(ang)
    cos128 = jnp.concatenate([cos, cos, cos, cos], axis=1)
    sin128 = jnp.concatenate([-sin, sin, -sin, sin], axis=1)
    return cos128, sin128


def _norm_rope(y, bd, gain, cos2, sin2, low):
    ms = _dot((y * y).astype(BF16), bd)
    yn = y * lax.rsqrt(ms + EPS) * gain
    up = pltpu.roll(yn, NR_CHUNK - HEAD_DIM // 2, axis=1)
    dn = pltpu.roll(yn, HEAD_DIM // 2, axis=1)
    return yn * cos2 + jnp.where(low, up, dn) * sin2


def _qkv_common(x_ref, mod_ref, ng_ref, cos_ref, sin_ref):
    shift, scale = mod_ref[0, 0:1, :], mod_ref[0, 1:2, :]
    h = _norm_mod(x_ref[0], ng_ref[...], scale, shift).astype(BF16)
    cos2 = jnp.concatenate([cos_ref[...]] * 2, axis=1)
    sin2 = jnp.concatenate([sin_ref[...]] * 2, axis=1)
    lane = lax.broadcasted_iota(jnp.int32, (1, NR_CHUNK), 1)
    low = (lane % HEAD_DIM) < HEAD_DIM // 2
    return h, cos2, sin2, low


def _swa_qkv_kernel(x_ref, mod_ref, ng_ref, w_ref, bd_ref, gq_ref, gk_ref, cos_ref, sin_ref,
                    q_ref, kv_ref):
    h, cos2, sin2, low = _qkv_common(x_ref, mod_ref, ng_ref, cos_ref, sin_ref)
    bd = bd_ref[...]
    nq = SWA_Q_HEADS * HEAD_DIM
    for c in range(nq // NR_CHUNK):
        y = _dot(h, w_ref[:, c * NR_CHUNK:(c + 1) * NR_CHUNK])
        q_ref[0, :, c * NR_CHUNK:(c + 1) * NR_CHUNK] = _norm_rope(
            y, bd, gq_ref[...], cos2, sin2, low).astype(BF16)
    k = _norm_rope(_dot(h, w_ref[:, nq:nq + NR_CHUNK]), bd, gk_ref[...], cos2, sin2, low)
    v = _dot(h, w_ref[:, nq + NR_CHUNK:nq + 2 * NR_CHUNK])
    lane = lax.broadcasted_iota(jnp.int32, (1, LANES), 1)
    lo = lane < HEAD_DIM
    k_cols, k_swaps, v_lo, v_hi = [], [], [], []
    for c in range(2):
        kc = k[:, c * LANES:(c + 1) * LANES]
        ks = pltpu.roll(kc, HEAD_DIM, axis=1)
        vc = v[:, c * LANES:(c + 1) * LANES]
        vs = pltpu.roll(vc, HEAD_DIM, axis=1)
        k_cols += [kc, ks]
        k_swaps += [ks, kc]
        v_lo += [jnp.where(lo, vc, 0.0), jnp.where(lo, vs, 0.0)]
        v_hi += [jnp.where(lo, 0.0, vs), jnp.where(lo, 0.0, vc)]
    kv_ref[0] = jnp.concatenate(k_cols + k_swaps + v_lo + v_hi, axis=1).astype(BF16)


def _diff_qkv_kernel(x_ref, mod_ref, ng_ref, wqt_ref, wk_ref, wvt_ref, bd_ref, gqt_ref, gk_ref,
                     cos_ref, sin_ref, cost_ref, sint_ref, qt_ref, k_ref, vt_ref):
    h, cos2, sin2, low = _qkv_common(x_ref, mod_ref, ng_ref, cos_ref, sin_ref)
    tm = h.shape[0]
    bd = bd_ref[...]
    n = DIFF_HEADS * 2 * HEAD_DIM
    heads = NR_CHUNK // HEAD_DIM
    half = HEAD_DIM // 2
    reps = tm // LANES
    gqt = jnp.concatenate([gqt_ref[...]] * reps, axis=1)
    cost, sint = cost_ref[...], sint_ref[...]
    for c in range(n // NR_CHUNK):
        yt = _dot_nt(wqt_ref[c * NR_CHUNK:(c + 1) * NR_CHUNK, :], h)
        y3 = yt.reshape(heads, HEAD_DIM, tm)
        ms = jnp.mean(y3 * y3, axis=1, keepdims=True)
        yn = y3 * lax.rsqrt(ms + EPS) * gqt
        partner = jnp.concatenate([yn[:, half:], yn[:, :half]], axis=1)
        qt = yn * cost + partner * sint
        qt_ref[0, 0, c * NR_CHUNK:(c + 1) * NR_CHUNK, :] = qt.reshape(NR_CHUNK, tm).astype(BF16)
    for c in range(n // NR_CHUNK):
        y = _dot(h, wk_ref[:, c * NR_CHUNK:(c + 1) * NR_CHUNK])
        k_ref[0, :, c * NR_CHUNK:(c + 1) * NR_CHUNK] = _norm_rope(
            y, bd, gk_ref[...], cos2, sin2, low).astype(BF16)
    vt_ref[0, 0] = _dot_nt(wvt_ref[...], h).astype(BF16)


def _diff_qkv(x, mod, ng, w, gq, gk, cos128, sin128, tm):
    B, S, D = x.shape
    n = DIFF_HEADS * 2 * HEAD_DIM
    nt = S // tm
    gq_s = gq.astype(F32) * (HEAD_DIM ** -0.5 * LOG2E)
    gqt = jnp.broadcast_to(gq_s.reshape(HEAD_DIM, 1), (HEAD_DIM, LANES))
    gk_t = jnp.tile(gk.astype(F32), NR_CHUNK // HEAD_DIM).reshape(1, NR_CHUNK)
    cost = cos128[:, :HEAD_DIM].T
    sint = sin128[:, :HEAD_DIM].T
    wqt = w[:, :n].T
    wk = w[:, n:2 * n]
    wvt = w[:, 2 * n:].T
    tshape = jax.ShapeDtypeStruct((B, nt, n, tm), BF16)
    return pl.pallas_call(
        _diff_qkv_kernel,
        grid=(B, nt),
        in_specs=[
            pl.BlockSpec((1, tm, D), lambda b, i: (b, i, 0)),
            pl.BlockSpec((1, 6, D), lambda b, i: (b, 0, 0)),
            _const_spec((1, D)),
            _const_spec((n, D)),
            _const_spec((D, n)),
            _const_spec((n, D)),
            _const_spec((NR_CHUNK, NR_CHUNK)),
            _const_spec((HEAD_DIM, LANES)),
            _const_spec((1, NR_CHUNK)),
            pl.BlockSpec((tm, LANES), lambda b, i: (i, 0)),
            pl.BlockSpec((tm, LANES), lambda b, i: (i, 0)),
            pl.BlockSpec((HEAD_DIM, tm), lambda b, i: (0, i)),
            pl.BlockSpec((HEAD_DIM, tm), lambda b, i: (0, i)),
        ],
        out_specs=[
            pl.BlockSpec((1, 1, n, tm), lambda b, i: (b, i, 0, 0)),
            pl.BlockSpec((1, tm, n), lambda b, i: (b, i, 0)),
            pl.BlockSpec((1, 1, n, tm), lambda b, i: (b, i, 0, 0)),
        ],
        out_shape=[tshape, jax.ShapeDtypeStruct((B, S, n), BF16), tshape],
        compiler_params=_cparams(("parallel", "parallel")),
        name="diff_qkv",
    )(x, mod, ng, wqt, wk, wvt, _seg_mean_matrix(), gqt, gk_t, cos128, sin128, cost, sint)


def _seg_mean_matrix():
    idx = np.arange(NR_CHUNK) // HEAD_DIM
    return jnp.asarray((idx[:, None] == idx[None, :]).astype(np.float32) / HEAD_DIM, BF16)


def _qkv(kernel, x, mod, ng, w, gq, gk, cos128, sin128, out_widths, tm=1024):
    B, S, D = x.shape
    scale = HEAD_DIM ** -0.5 * LOG2E
    gq_t = jnp.tile(gq.astype(F32) * scale, NR_CHUNK // HEAD_DIM).reshape(1, NR_CHUNK)
    gk_t = jnp.tile(gk.astype(F32), NR_CHUNK // HEAD_DIM).reshape(1, NR_CHUNK)
    return pl.pallas_call(
        kernel,
        grid=(B, S // tm),
        in_specs=[
            pl.BlockSpec((1, tm, D), lambda b, i: (b, i, 0)),
            pl.BlockSpec((1, 6, D), lambda b, i: (b, 0, 0)),
            _const_spec((1, D)),
            _const_spec(w.shape),
            _const_spec((NR_CHUNK, NR_CHUNK)),
            _const_spec((1, NR_CHUNK)),
            _const_spec((1, NR_CHUNK)),
            pl.BlockSpec((tm, LANES), lambda b, i: (i, 0)),
            pl.BlockSpec((tm, LANES), lambda b, i: (i, 0)),
        ],
        out_specs=[pl.BlockSpec((1, tm, n), lambda b, i: (b, i, 0)) for n in out_widths],
        out_shape=[jax.ShapeDtypeStruct((B, S, n), BF16) for n in out_widths],
        compiler_params=_cparams(("parallel", "parallel")),
        name=kernel.__name__.strip("_"),
    )(x, mod, ng, w, _seg_mean_matrix(), gq_t, gk_t, cos128, sin128)


def _swa_attn_kernel(sink_ref, q_ref, kvp_ref, kvm_ref, kvn_ref, x_ref, mod_ref, wo_ref, o_ref,
                     kv_scr, o_scr, *, tq, seq):
    i = pl.program_id(1)
    w = WINDOW
    kv_scr[0:w, :] = kvp_ref[0]
    kv_scr[w:w + tq, :] = kvm_ref[0]
    kv_scr[w + tq:w + tq + w, :] = kvn_ref[0]
    lane = lax.broadcasted_iota(jnp.int32, (1, LANES), 1)
    lo = lane < HEAD_DIM
    nk = 3 * w
    row = lax.broadcasted_iota(jnp.int32, (2 * w, nk), 0) % w
    col = lax.broadcasted_iota(jnp.int32, (2 * w, nk), 1) - w
    band = jnp.abs(row - col) <= WINDOW
    top = lax.broadcasted_iota(jnp.int32, (2 * w, 1), 0) < w
    kgrp = SWA_KV_HEADS * LANES

    def block(n, carry):
        r0 = pl.multiple_of(n * w, w)
        kpos = i * tq + n * w + col
        bias = jnp.where(band & (kpos >= 0) & (kpos < seq), 0.0, NEG)
        chains = [(hk, half) for hk in range(SWA_KV_HEADS) for half in range(2)]

        def scores(hk, half):
            c0 = 2 * hk * LANES
            qs = jnp.concatenate([q_ref[0, pl.ds(r0, w), c0:c0 + LANES],
                                  q_ref[0, pl.ds(r0, w), c0 + LANES:c0 + 2 * LANES]], axis=0)
            zero = jnp.zeros_like(qs)
            qm = jnp.where(lo, zero, qs) if half else jnp.where(lo, qs, zero)
            kc = half * kgrp + hk * LANES
            return _dot_nt(qm, kv_scr[pl.ds(r0, nk), kc:kc + LANES]) + bias

        ahead = scores(*chains[0])
        acc = None
        for idx, (hk, half) in enumerate(chains):
            s = ahead
            if idx + 1 < len(chains):
                ahead = scores(*chains[idx + 1])
            vc = (2 + half) * kgrp + hk * LANES
            vm = kv_scr[pl.ds(r0, nk), vc:vc + LANES]
            sk = jnp.where(top, sink_ref[4 * hk + half], sink_ref[4 * hk + 2 + half]) * LOG2E
            m = jnp.maximum(jnp.max(s, axis=-1, keepdims=True), sk)
            e = jnp.exp2(s - m)
            den = jnp.sum(e, axis=-1, keepdims=True) + jnp.exp2(sk - m)
            part = _dot(e.astype(BF16), vm) * (1.0 / den)
            if half == 0:
                acc = part
            else:
                acc = acc + part
                c0 = 2 * hk * LANES
                o_scr[pl.ds(r0, w), c0:c0 + LANES] = acc[:w].astype(BF16)
                o_scr[pl.ds(r0, w), c0 + LANES:c0 + 2 * LANES] = acc[w:].astype(BF16)
        return carry

    lax.fori_loop(0, tq // w, block, 0)
    y = _dot(o_scr[...], wo_ref[...])
    o_ref[0] = x_ref[0] + mod_ref[0, 2:3, :] * y


def _swa_attn(q, kv, x, mod, sink, wo, tq=1024):
    B, S, D = x.shape
    w = WINDOW
    r = tq // w
    nkv = kv.shape[-1]
    return pl.pallas_call(
        functools.partial(_swa_attn_kernel, tq=tq, seq=S),
        grid=(B, S // tq),
        in_specs=[
            pl.BlockSpec(memory_space=pltpu.SMEM),
            pl.BlockSpec((1, tq, D), lambda b, i: (b, i, 0)),
            pl.BlockSpec((1, w, nkv), lambda b, i: (b, jnp.maximum(i * r - 1, 0), 0)),
            pl.BlockSpec((1, tq, nkv), lambda b, i: (b, i, 0)),
            pl.BlockSpec((1, w, nkv), lambda b, i: (b, jnp.minimum((i + 1) * r, S // w - 1), 0)),
            pl.BlockSpec((1, tq, D), lambda b, i: (b, i, 0)),
            pl.BlockSpec((1, 6, D), lambda b, i: (b, 0, 0)),
            _const_spec((D, D)),
        ],
        out_specs=pl.BlockSpec((1, tq, D), lambda b, i: (b, i, 0)),
        out_shape=jax.ShapeDtypeStruct((B, S, D), F32),
        scratch_shapes=[pltpu.VMEM((tq + 2 * w, nkv), BF16), pltpu.VMEM((tq, D), BF16)],
        compiler_params=_cparams(("parallel", "parallel")),
        name="swa_attention",
    )(sink, q, kv, kv, kv, x, mod, wo)


def _diff_attn_kernel(qt_ref, k_ref, vt_ref, lvec_ref, sgt_ref, o_ref, acc_scr, s_scr,
                      *, tq, tk, seq, lambda_init):
    rowi = lax.broadcasted_iota(jnp.int32, (LANES, 1), 0)
    first = rowi < HEAD_DIM
    qt = qt_ref[0, 0]
    zero = jnp.zeros_like(qt)
    qc = (jnp.where(first, qt, zero), jnp.where(first, zero, qt))
    acc_scr[...] = jnp.zeros(acc_scr.shape, F32)
    nsub = tq // DIFF_SUB
    chains = [(c, hq * DIFF_SUB) for c in range(2) for hq in range(nsub)]
    qsub = [qc[c][:, q0:q0 + DIFF_SUB] for c, q0 in chains]
    n = seq // tk

    def scores(j, slot, i):
        c, q0 = chains[i]
        k0 = pl.multiple_of(j * tk, tk)
        kb = k_ref[0, pl.ds(k0, tk), :]
        s = _dot(kb, qsub[i])
        s_scr[slot, c, :, q0:q0 + DIFF_SUB] = s
        return jnp.max(s, axis=0, keepdims=True)

    def probs_acc(j, slot, i, mcur, m, l):
        c, q0 = chains[i]
        half = (slot % (tq // tk)) * tk
        vb = vt_ref[0, j // (tq // tk), :, half:half + tk]
        m_new = jnp.maximum(m, mcur)
        alpha = jnp.exp2(m - m_new)
        p = jnp.exp2(s_scr[slot, c, :, q0:q0 + DIFF_SUB] - m_new)
        l_new = alpha * l + jnp.sum(p, axis=0, keepdims=True)
        acc_scr[c, :, q0:q0 + DIFF_SUB] = (alpha * acc_scr[c, :, q0:q0 + DIFF_SUB]
                                           + _dot(vb, p.astype(BF16)))
        return m_new, l_new

    nc = len(chains)
    ms = tuple(jnp.full((1, DIFF_SUB), NEG, F32) for _ in chains)
    ls = tuple(jnp.zeros((1, DIFF_SUB), F32) for _ in chains)
    mcur = tuple(scores(0, 0, i) for i in range(nc))

    def body(j, slot, mcur, ms, ls):
        mc_new, ms_new, ls_new = [None] * nc, list(ms), list(ls)
        mc_new[0] = scores(j, slot, 0)
        for i in range(nc):
            if i + 1 < nc:
                mc_new[i + 1] = scores(j, slot, i + 1)
            ms_new[i], ls_new[i] = probs_acc(j - 1, 1 - slot, i, mcur[i], ms[i], ls[i])
        return tuple(mc_new), tuple(ms_new), tuple(ls_new)

    mcur, ms, ls = body(1, 1, mcur, ms, ls)

    def pair(t, carry):
        mcur, ms, ls = carry
        mcur, ms, ls = body(2 * t, 0, mcur, ms, ls)
        mcur, ms, ls = body(2 * t + 1, 1, mcur, ms, ls)
        return mcur, ms, ls

    mcur, ms, ls = lax.fori_loop(1, n // 2, pair, (mcur, ms, ls))
    ms, ls = list(ms), list(ls)
    for i in range(nc):
        ms[i], ls[i] = probs_acc(n - 1, 1, i, mcur[i], ms[i], ls[i])
    l0 = jnp.concatenate(ls[:nsub], axis=1)
    l1 = jnp.concatenate(ls[nsub:], axis=1)
    lv = lvec_ref[...]
    lam = (jnp.exp(jnp.sum(lv[0:1] * lv[1:2], axis=-1, keepdims=True))
           - jnp.exp(jnp.sum(lv[2:3] * lv[3:4], axis=-1, keepdims=True)) + lambda_init)
    ot = acc_scr[0] / l0 - lam * (acc_scr[1] / l1)
    ms = jnp.mean(ot * ot, axis=0, keepdims=True)
    sgt = jnp.concatenate([sgt_ref[...]] * (tq // LANES), axis=1)
    ot = ot * lax.rsqrt(ms + EPS) * sgt * (1.0 - lambda_init)
    o_ref[0] = ot.T.astype(BF16)


def _diff_attn(qt, k, vt, lvec, sgt, lambda_init, tq, tk):
    B, S, D = k.shape
    assert tq // tk in (1, 2) and tq % tk == 0
    nchunk = S // tq
    return pl.pallas_call(
        functools.partial(_diff_attn_kernel, tq=tq, tk=tk, seq=S, lambda_init=lambda_init),
        grid=(B, DIFF_HEADS, S // tq),
        in_specs=[
            pl.BlockSpec((1, 1, LANES, tq), lambda b, h, i: (b, i, h, 0)),
            pl.BlockSpec((1, S, LANES), lambda b, h, i: (b, 0, h)),
            pl.BlockSpec((1, nchunk, LANES, tq), lambda b, h, i: (b, 0, h, 0)),
            _const_spec((4, LANES)),
            _const_spec((LANES, LANES)),
        ],
        out_specs=pl.BlockSpec((1, tq, LANES), lambda b, h, i: (b, i, h)),
        out_shape=jax.ShapeDtypeStruct((B, S, D), BF16),
        scratch_shapes=[pltpu.VMEM((2, LANES, tq), F32), pltpu.VMEM((2, 2, tk, tq), F32)],
        compiler_params=_cparams(("parallel", "parallel", "parallel")),
        name="diff_attention",
    )(qt, k, vt, lvec, sgt)


def _proj_res_kernel(o_ref, wo_ref, x_ref, mod_ref, out_ref):
    out_ref[0] = x_ref[0] + mod_ref[0, 2:3, :] * _dot(o_ref[0], wo_ref[...])


def _proj_res(o, wo, x, mod, tm=1024):
    B, S, D = x.shape
    return pl.pallas_call(
        _proj_res_kernel,
        grid=(B, S // tm),
        in_specs=[
            pl.BlockSpec((1, tm, D), lambda b, i: (b, i, 0)),
            _const_spec((D, D)),
            pl.BlockSpec((1, tm, D), lambda b, i: (b, i, 0)),
            pl.BlockSpec((1, 6, D), lambda b, i: (b, 0, 0)),
        ],
        out_specs=pl.BlockSpec((1, tm, D), lambda b, i: (b, i, 0)),
        out_shape=jax.ShapeDtypeStruct((B, S, D), F32),
        compiler_params=_cparams(("parallel", "parallel")),
        name="out_proj_residual",
    )(o, wo, x, mod)


def _trunk(x, mods, p):
    S = x.shape[1]
    cos128, sin128 = _rope_tables(S)
    fconsts = _fnet_consts(S)
    for i in range(DEPTH):
        mod = mods[i]
        ng1 = p["norm1_g"][i].reshape(1, D_MODEL)
        ng2 = p["norm2_g"][i].reshape(1, D_MODEL)
        kind, j = i % 3, i // 3
        if kind == 0:
            x = _fnet(x, mod, ng1, p["fnet_w"][j], p["fnet_b"][j].reshape(1, D_MODEL), fconsts)
        elif kind == 1:
            q, kv = _qkv(_swa_qkv_kernel, x, mod, ng1, p["swa_w_qkv"][j], p["swa_q_g"][j],
                         p["swa_k_g"][j], cos128, sin128,
                         (SWA_Q_HEADS * HEAD_DIM, 4 * SWA_KV_HEADS * LANES))
            x = _swa_attn(q, kv, x, mod, p["swa_sink"][j], p["swa_w_o"][j])
        else:
            lambda_init = 0.8 - 0.6 * math.exp(-0.3 * i)
            qt, k, vt = _diff_qkv(x, mod, ng1, p["diff_w_qkv"][j], p["diff_q_g"][j],
                                  p["diff_k_g"][j], cos128, sin128, DIFF_TILE)
            pad = ((0, 0), (0, LANES - HEAD_DIM))
            lvec = jnp.pad(jnp.stack([p["diff_lq1"][j], p["diff_lk1"][j],
                                      p["diff_lq2"][j], p["diff_lk2"][j]]).astype(F32), pad)
            sgt = jnp.broadcast_to(p["diff_subln_g"][j].astype(F32).reshape(LANES, 1),
                                   (LANES, LANES))
            tk = DIFF_TILE if S // DIFF_TILE >= 4 else DIFF_TILE // 2
            o = _diff_attn(qt, k, vt, lvec, sgt, lambda_init, DIFF_TILE, tk)
            x = _proj_res(o, p["diff_w_o"][j], x, mod)
        x = _ffn(x, mod, ng2, p["ffn_w_gate"][i], p["ffn_w_up"][i], p["ffn_conv_w"][i],
                 p["ffn_conv_b"][i].reshape(1, D_FF), p["ffn_w_down"][i])
    return x


def kernel(x_prompt, x_sample, c_prompt, c_sample, ada_w, ada_b, norm1_g, norm2_g, fnet_w, fnet_b, swa_w_qkv, swa_q_g, swa_k_g, swa_sink, swa_w_o, diff_w_qkv, diff_q_g, diff_k_g, diff_lq1, diff_lk1, diff_lq2, diff_lk2, diff_subln_g, diff_w_o, ffn_w_gate, ffn_w_up, ffn_conv_w, ffn_conv_b, ffn_w_down):
    bp, bs = c_prompt.shape[0], c_sample.shape[0]
    rows = -(-(bp + bs) // SUBLANES) * SUBLANES
    c_all = jnp.concatenate(
        [c_prompt, c_sample, jnp.zeros((rows - bp - bs, D_MODEL), F32)], axis=0)
    mods = _ada_mod(c_all, ada_w, ada_b).reshape(DEPTH, rows, 6, D_MODEL)
    p = {
        "norm1_g": norm1_g, "norm2_g": norm2_g,
        "fnet_w": fnet_w.astype(BF16), "fnet_b": fnet_b,
        "swa_w_qkv": swa_w_qkv.astype(BF16), "swa_q_g": swa_q_g, "swa_k_g": swa_k_g,
        "swa_sink": swa_sink, "swa_w_o": swa_w_o.astype(BF16),
        "diff_w_qkv": diff_w_qkv.astype(BF16), "diff_q_g": diff_q_g, "diff_k_g": diff_k_g,
        "diff_lq1": diff_lq1, "diff_lk1": diff_lk1, "diff_lq2": diff_lq2, "diff_lk2": diff_lk2,
        "diff_subln_g": diff_subln_g, "diff_w_o": diff_w_o.astype(BF16),
        "ffn_w_gate": ffn_w_gate.astype(BF16), "ffn_w_up": ffn_w_up.astype(BF16),
        "ffn_conv_w": ffn_conv_w, "ffn_conv_b": ffn_conv_b,
        "ffn_w_down": ffn_w_down.astype(BF16),
    }
    y_prompt = _trunk(x_prompt, mods[:, :bp], p)
    y_sample = _trunk(x_sample, mods[:, bp:bp + bs], p)
    return (y_prompt, y_sample)
```
